```python
import math
import jax, jax.numpy as jnp
from jax import lax
import numpy as np

D_MODEL = 2048
BATCH = 4
SEQ = 4096
DEPTH = 1

HEAD_DIM = 128
N_HEADS_TOTAL = D_MODEL // HEAD_DIM
MLA_HEADS = N_HEADS_TOTAL // 2
MOBA_HEADS = N_HEADS_TOTAL - MLA_HEADS
MLA_Q_RANK = 3 * D_MODEL // 16
MLA_KV_RANK = D_MODEL // 8
MLA_NOPE_DIM = HEAD_DIM
MLA_ROPE_DIM = HEAD_DIM // 2
MLA_V_DIM = HEAD_DIM
MOBA_BLOCK = 256
MOBA_TOPK = 3
MOBA_Q_CHUNK = 16
ATTN_Q_BLOCK = 128
ROPE_THETA = 500000.0
PARTIAL_ROT_DIM = HEAD_DIM // 4
D_FF = 4 * D_MODEL
LN_EPS = 1e-5
RMS_EPS = 1e-6
DEEPNORM_ALPHA = (2 * DEPTH) ** 0.25
DEEPNORM_BETA = (8 * DEPTH) ** -0.25
MOBA_WIDTH = MOBA_HEADS * HEAD_DIM
IN_SPLITS = (MLA_Q_RANK, MLA_KV_RANK, MLA_ROPE_DIM, MOBA_WIDTH, MOBA_WIDTH, MOBA_WIDTH)
D_IN = sum(IN_SPLITS)
MLA_SCALE = 1.0 / math.sqrt(MLA_NOPE_DIM + MLA_ROPE_DIM)
MOBA_SCALE = 1.0 / math.sqrt(HEAD_DIM)

kernel_name = "hybrid_mla_moba_deepnorm_layer"


def layer_norm(x, g, b):
    xf = x.astype(jnp.float32)
    mu = jnp.mean(xf, axis=-1, keepdims=True)
    var = jnp.mean(jnp.square(xf - mu), axis=-1, keepdims=True)
    y = (xf - mu) * lax.rsqrt(var + LN_EPS) * g.astype(jnp.float32) + b.astype(jnp.float32)
    return y.astype(x.dtype)


def rms_norm(x, g):
    xf = x.astype(jnp.float32)
    y = xf * lax.rsqrt(jnp.mean(jnp.square(xf), axis=-1, keepdims=True) + RMS_EPS)
    return (y * g.astype(jnp.float32)).astype(x.dtype)


def apply_rope(x, positions):
    d = x.shape[-1]
    half = d // 2
    inv_freq = ROPE_THETA ** (-jnp.arange(half, dtype=jnp.float32) * (2.0 / d))
    ang = positions.astype(jnp.float32)[:, :, None, None] * inv_freq
    cos, sin = jnp.cos(ang), jnp.sin(ang)
    xf = x.astype(jnp.float32)
    x1, x2 = xf[..., :half], xf[..., half:]
    return jnp.concatenate([x1 * cos - x2 * sin, x2 * cos + x1 * sin], axis=-1).astype(x.dtype)


def causal_dense_attention(q, k, v, scale):
    B, S, H, _ = q.shape
    dv = v.shape[-1]
    q, k, v = (t.transpose(0, 2, 1, 3) for t in (q, k, v))
    kpos = jnp.arange(S)

    def one_block(i):
        start = i * ATTN_Q_BLOCK
        qb = lax.dynamic_slice_in_dim(q, start, ATTN_Q_BLOCK, axis=2)
        s = jnp.einsum('bhqd,bhkd->bhqk', qb, k, preferred_element_type=jnp.float32) * scale
        qpos = start + jnp.arange(ATTN_Q_BLOCK)
        s = jnp.where(kpos[None, :] <= qpos[:, None], s, -jnp.inf)
        p = jax.nn.softmax(s, axis=-1).astype(v.dtype)
        return jnp.einsum('bhqk,bhkd->bhqd', p, v)

    out = lax.map(one_block, jnp.arange(S // ATTN_Q_BLOCK))
    return out.transpose(1, 0, 3, 2, 4).reshape(B, S, H * dv)


def mla_group(c_q, c_kv, k_rope_in, positions, q_norm, kv_norm, w_uq, w_ukv):
    B, S, _ = c_q.shape
    q = (rms_norm(c_q, q_norm) @ w_uq).reshape(B, S, MLA_HEADS, MLA_NOPE_DIM + MLA_ROPE_DIM)
    q_nope, q_rope = q[..., :MLA_NOPE_DIM], q[..., MLA_NOPE_DIM:]
    q = jnp.concatenate([q_nope, apply_rope(q_rope, positions)], axis=-1)
    kv = (rms_norm(c_kv, kv_norm) @ w_ukv).reshape(B, S, MLA_HEADS, MLA_NOPE_DIM + MLA_V_DIM)
    k_nope, v = kv[..., :MLA_NOPE_DIM], kv[..., MLA_NOPE_DIM:]
    k_rope = apply_rope(k_rope_in[:, :, None, :], positions)
    k = jnp.concatenate([k_nope, jnp.broadcast_to(k_rope, (B, S, MLA_HEADS, MLA_ROPE_DIM))], axis=-1)
    return causal_dense_attention(q, k, v, MLA_SCALE)


def partial_rope(x, positions):
    return jnp.concatenate([apply_rope(x[..., :PARTIAL_ROT_DIM], positions), x[..., PARTIAL_ROT_DIM:]], axis=-1)


def moba_group(q, k, v, positions):
    B, S, H, D = q.shape
    L = MOBA_BLOCK
    q = partial_rope(q, positions).transpose(0, 2, 1, 3)
    k = partial_rope(k, positions).transpose(0, 2, 1, 3)
    v = v.transpose(0, 2, 1, 3)
    n_blocks = -(-S // L)
    s_pad = n_blocks * L
    pad = ((0, 0), (0, 0), (0, s_pad - S), (0, 0))
    k_pad, v_pad = jnp.pad(k, pad), jnp.pad(v, pad)
    kb = k_pad.reshape(B, H, n_blocks, L, D)
    vb = v_pad.reshape(B, H, n_blocks, L, D)
    k_mean = jnp.mean(kb.astype(jnp.float32), axis=3).astype(q.dtype)
    gate = jnp.einsum('bhsd,bhnd->bhsn', q, k_mean, preferred_element_type=jnp.float32)
    q_blk = jnp.arange(S) // L
    past = jnp.arange(n_blocks)[None, :] < q_blk[:, None]
    gate = jnp.where(past, gate, -jnp.inf)
    topk = min(MOBA_TOPK, n_blocks)
    _, sel_idx = lax.top_k(gate, topk)
    sel_valid = sel_idx < q_blk[:, None]
    gather_blocks = jax.vmap(jax.vmap(lambda blocks, ix: blocks[ix]))

    def one_chunk(c):
        start = c * MOBA_Q_CHUNK
        qc = lax.dynamic_slice_in_dim(q, start, MOBA_Q_CHUNK, axis=2)
        ic = lax.dynamic_slice_in_dim(sel_idx, start, MOBA_Q_CHUNK, axis=2)
        vc = lax.dynamic_slice_in_dim(sel_valid, start, MOBA_Q_CHUNK, axis=2)
        k_sel = gather_blocks(kb, ic)
        v_sel = gather_blocks(vb, ic)
        s_sel = jnp.einsum('bhqd,bhqjld->bhqjl', qc, k_sel, preferred_element_type=jnp.float32) * MOBA_SCALE
        s_sel = jnp.where(vc[..., None], s_sel, -jnp.inf).reshape(B, H, MOBA_Q_CHUNK, topk * L)
        own_start = (start // L) * L
        k_own = lax.dynamic_slice_in_dim(k_pad, own_start, L, axis=2)
        v_own = lax.dynamic_slice_in_dim(v_pad, own_start, L, axis=2)
        s_own = jnp.einsum('bhqd,bhld->bhql', qc, k_own, preferred_element_type=jnp.float32) * MOBA_SCALE
        qpos = start + jnp.arange(MOBA_Q_CHUNK)
        kpos = own_start + jnp.arange(L)
        s_own = jnp.where(kpos[None, :] <= qpos[:, None], s_own, -jnp.inf)
        p = jax.nn.softmax(jnp.concatenate([s_sel, s_own], axis=-1), axis=-1).astype(v.dtype)
        p_sel = p[..., :topk * L].reshape(B, H, MOBA_Q_CHUNK, topk, L)
        p_own = p[..., topk * L:]
        return (jnp.einsum('bhqjl,bhqjld->bhqd', p_sel, v_sel)
                + jnp.einsum('bhql,bhld->bhqd', p_own, v_own))

    out = lax.map(one_chunk, jnp.arange(S // MOBA_Q_CHUNK))
    return out.transpose(1, 0, 3, 2, 4).reshape(B, S, H * D)


def setup_inputs(seed: int = 0) -> dict:
    key = jax.random.key(seed)
    ks = jax.random.split(key, 16)
    f32 = jnp.float32
    nrm = lambda k, shape, fan_in, gain=1.0: jax.random.normal(k, shape, f32) * (gain * fan_in ** -0.5)
    x = jax.random.normal(ks[0], (BATCH, SEQ, D_MODEL), f32)
    offsets = jax.random.randint(ks[1], (BATCH, 1), 0, 1024, dtype=jnp.int32)
    positions = (offsets + jnp.arange(SEQ, dtype=jnp.int32)[None, :]).astype(jnp.int32)
    w_in = nrm(ks[2], (DEPTH, D_MODEL, D_IN), D_MODEL)
    mla_q_norm = 1.0 + 0.02 * jax.random.normal(ks[3], (DEPTH, MLA_Q_RANK), f32)
    mla_kv_norm = 1.0 + 0.02 * jax.random.normal(ks[4], (DEPTH, MLA_KV_RANK), f32)
    w_uq = nrm(ks[5], (DEPTH, MLA_Q_RANK, MLA_HEADS * (MLA_NOPE_DIM + MLA_ROPE_DIM)), MLA_Q_RANK)
    w_ukv = nrm(ks[6], (DEPTH, MLA_KV_RANK, MLA_HEADS * (MLA_NOPE_DIM + MLA_V_DIM)), MLA_KV_RANK)
    w_out = nrm(ks[7], (DEPTH, MLA_HEADS * MLA_V_DIM + MOBA_WIDTH, D_MODEL), D_MODEL, DEEPNORM_BETA)
    ln1_g = 1.0 + 0.02 * jax.random.normal(ks[8], (DEPTH, D_MODEL), f32)
    ln1_b = 0.02 * jax.random.normal(ks[9], (DEPTH, D_MODEL), f32)
    w_up = nrm(ks[10], (DEPTH, D_MODEL, D_FF), D_MODEL)
    w_down = nrm(ks[11], (DEPTH, D_FF, D_MODEL), D_FF, DEEPNORM_BETA)
    ln2_g = 1.0 + 0.02 * jax.random.normal(ks[12], (DEPTH, D_MODEL), f32)
    ln2_b = 0.02 * jax.random.normal(ks[13], (DEPTH, D_MODEL), f32)
    return {"x": x, "positions": positions, "w_in": w_in, "mla_q_norm": mla_q_norm,
            "mla_kv_norm": mla_kv_norm, "w_uq": w_uq, "w_ukv": w_ukv, "w_out": w_out,
            "ln1_g": ln1_g, "ln1_b": ln1_b, "w_up": w_up, "w_down": w_down,
            "ln2_g": ln2_g, "ln2_b": ln2_b}


def reference(x, positions, w_in, mla_q_norm, mla_kv_norm, w_uq, w_ukv, w_out,
              ln1_g, ln1_b, w_up, w_down, ln2_g, ln2_b):
    B, S, _ = x.shape
    cuts = list(np.cumsum(IN_SPLITS)[:-1])
    for l in range(DEPTH):
        h = x @ w_in[l]
        c_q, c_kv, k_r, m_q, m_k, m_v = jnp.split(h, cuts, axis=-1)
        mla_out = mla_group(c_q, c_kv, k_r, positions, mla_q_norm[l], mla_kv_norm[l], w_uq[l], w_ukv[l])
        rs = lambda t: t.reshape(B, S, MOBA_HEADS, HEAD_DIM)
        moba_out = moba_group(rs(m_q), rs(m_k), rs(m_v), positions)
        mix = jnp.concatenate([mla_out, moba_out], axis=-1) @ w_out[l]
        x = layer_norm(DEEPNORM_ALPHA * x + mix, ln1_g[l], ln1_b[l])
        ff = jnp.square(jax.nn.relu(x @ w_up[l])) @ w_down[l]
        x = layer_norm(DEEPNORM_ALPHA * x + ff, ln2_g[l], ln2_b[l])
    return x
```

```python
import functools
import math

import numpy as np
import jax
import jax.numpy as jnp
from jax import lax
from jax.experimental import pallas as pl
from jax.experimental.pallas import tpu as pltpu

HEAD_DIM = 128
MLA_HEADS = 8
MOBA_HEADS = 8
MLA_Q_RANK = 384
MLA_KV_RANK = 256
MLA_ROPE_DIM = 64
MOBA_BLOCK = 256
MOBA_TOPK = 3
ROPE_THETA = 500000.0
PARTIAL_ROT_DIM = 32
LN_EPS = 1e-5
RMS_EPS = 1e-6
DEPTH = 1
DEEPNORM_ALPHA = (2 * DEPTH) ** 0.25
MLA_SCALE = 1.0 / math.sqrt(HEAD_DIM + MLA_ROPE_DIM)
MOBA_SCALE = 1.0 / math.sqrt(HEAD_DIM)
LOG2E = math.log2(math.e)

LANES = 128
MXU_DTYPE = jnp.bfloat16
VMEM_LIMIT_BYTES = 56 * 1024 * 1024

ATTN_TILE = MOBA_BLOCK
MLA_PREP_ROWS = 512
MOBA_PREP_ROWS = 1024
OUT_ROWS = 512
FFN_ROWS = 512
FFN_COLS = 1024
TABLE_ROWS = 1024


def _dot(a, b):
    return jnp.dot(a, b, preferred_element_type=jnp.float32)


def _params(*sem):
    return pltpu.CompilerParams(dimension_semantics=sem, vmem_limit_bytes=VMEM_LIMIT_BYTES)


def _tables_kernel(pos_ref, invf_ref, tmla_ref, cf_ref, s1_ref, s2_ref):
    pos = pos_ref[...].astype(jnp.float32)
    lane = lax.broadcasted_iota(jnp.int32, tmla_ref.shape, 1)
    ang = pos * invf_ref[0:1, :]
    c, s = jnp.cos(ang), jnp.sin(ang)
    tmla_ref[...] = jnp.where(lane < 64, c, jnp.where(lane < 96, -s, s))
    ang2 = pos * invf_ref[1:2, :]
    c2, s2 = jnp.cos(ang2), jnp.sin(ang2)
    half = PARTIAL_ROT_DIM // 2
    cf_ref[...] = jnp.where(lane < PARTIAL_ROT_DIM, c2, 1.0)
    s1_ref[...] = jnp.where((lane >= half) & (lane < PARTIAL_ROT_DIM), s2, 0.0)
    s2_ref[...] = jnp.where(lane < half, -s2, 0.0)


def _rope_tables(positions):
    B, S = positions.shape
    M = B * S
    rows = min(TABLE_ROWS, M)
    f32 = jnp.float32
    half_a = MLA_ROPE_DIM // 2
    invf_a = ROPE_THETA ** (-jnp.arange(half_a, dtype=f32) * (2.0 / MLA_ROPE_DIM))
    half_b = PARTIAL_ROT_DIM // 2
    invf_b = ROPE_THETA ** (-jnp.arange(half_b, dtype=f32) * (2.0 / PARTIAL_ROT_DIM))
    row_a = jnp.tile(invf_a, LANES // half_a)
    row_b = jnp.concatenate([invf_b, invf_b, jnp.zeros((LANES - PARTIAL_ROT_DIM,), f32)])
    invf = jnp.stack([row_a, row_b])
    out = jax.ShapeDtypeStruct((M, LANES), f32)
    spec = pl.BlockSpec((rows, LANES), lambda i: (i, 0))
    tabs = pl.pallas_call(
        _tables_kernel,
        grid=(M // rows,),
        in_specs=[pl.BlockSpec((rows, 1), lambda i: (i, 0)), pl.BlockSpec((2, LANES), lambda i: (0, 0))],
        out_specs=[spec] * 4,
        out_shape=[out] * 4,
        compiler_params=_params("parallel"),
        name="rope_tables",
    )(positions.reshape(M, 1), invf)
    return [t.reshape(B, S, LANES) for t in tabs]


def _rms(x, g):
    y = x * lax.rsqrt(jnp.mean(jnp.square(x), axis=-1, keepdims=True) + RMS_EPS)
    return y * g


def _mla_prep_kernel(x_ref, wlat_ref, gq_ref, gkv_ref, wuq_ref, wukv_ref, t_ref, qT_ref, k_ref, vT_ref):
    rows = x_ref.shape[1]
    nblk = rows // ATTN_TILE
    xb = x_ref[0].astype(MXU_DTYPE)
    lat = _dot(xb, wlat_ref[...])
    cq = lat[:, :MLA_Q_RANK]
    ckv = lat[:, MLA_Q_RANK:MLA_Q_RANK + MLA_KV_RANK]
    kr2 = lat[:, MLA_Q_RANK + MLA_KV_RANK:]
    tab = t_ref[0]
    lane = lax.broadcasted_iota(jnp.int32, tab.shape, 1)
    t = kr2 * tab
    k_rope = jnp.where(lane < MLA_ROPE_DIM, t + pltpu.roll(t, MLA_ROPE_DIM, 1), 0.0).astype(MXU_DTYPE)
    qall = _dot(_rms(cq, gq_ref[...]).astype(MXU_DTYPE), wuq_ref[...])
    kvall = _dot(_rms(ckv, gkv_ref[...]).astype(MXU_DTYPE), wukv_ref[...])
    dq = 2 * HEAD_DIM
    for h in range(MLA_HEADS):
        nope = qall[:, h * dq:h * dq + HEAD_DIM]
        t = qall[:, h * dq + HEAD_DIM:(h + 1) * dq] * tab
        rope = t + pltpu.roll(t, MLA_ROPE_DIM, 1)
        for blk in range(nblk):
            r = slice(blk * ATTN_TILE, (blk + 1) * ATTN_TILE)
            qT_ref[0, h, blk, 0:HEAD_DIM, :] = nope[r].T.astype(MXU_DTYPE)
            qT_ref[0, h, blk, HEAD_DIM:dq, :] = rope[r].T.astype(MXU_DTYPE)
        k_ref[0, h, :, 0:HEAD_DIM] = kvall[:, h * HEAD_DIM:(h + 1) * HEAD_DIM].astype(MXU_DTYPE)
        k_ref[0, h, :, HEAD_DIM:dq] = k_rope
        v = kvall[:, (MLA_HEADS + h) * HEAD_DIM:(MLA_HEADS + h + 1) * HEAD_DIM]
        for blk in range(nblk):
            r = slice(blk * ATTN_TILE, (blk + 1) * ATTN_TILE)
            vT_ref[0, h, blk] = v[r].T.astype(MXU_DTYPE)


def _mla_prep(x, w_lat, gq, gkv, wuq, wukv, tmla):
    B, S, D = x.shape
    rows = min(MLA_PREP_ROWS, S)
    nb = S // ATTN_TILE
    H, dq = MLA_HEADS, 2 * HEAD_DIM
    full = lambda a: pl.BlockSpec(a.shape, lambda b, i: (0,) * a.ndim)
    return pl.pallas_call(
        _mla_prep_kernel,
        grid=(B, S // rows),
        in_specs=[pl.BlockSpec((1, rows, D), lambda b, i: (b, i, 0)),
                  full(w_lat), full(gq), full(gkv), full(wuq), full(wukv),
                  pl.BlockSpec((1, rows, LANES), lambda b, i: (b, i, 0))],
        out_specs=[pl.BlockSpec((1, H, rows // ATTN_TILE, dq, ATTN_TILE), lambda b, i: (b, 0, i, 0, 0)),
                   pl.BlockSpec((1, H, rows, dq), lambda b, i: (b, 0, i, 0)),
                   pl.BlockSpec((1, H, rows // ATTN_TILE, HEAD_DIM, ATTN_TILE), lambda b, i: (b, 0, i, 0, 0))],
        out_shape=[jax.ShapeDtypeStruct((B, H, nb, dq, ATTN_TILE), MXU_DTYPE),
                   jax.ShapeDtypeStruct((B, H, S, dq), MXU_DTYPE),
                   jax.ShapeDtypeStruct((B, H, nb, HEAD_DIM, ATTN_TILE), MXU_DTYPE)],
        compiler_params=_params("parallel", "parallel"),
        name="mla_prep",
    )(x, w_lat, gq, gkv, wuq, wukv, tmla)


def _moba_prep_kernel(x_ref, w_ref, cf_ref, s1_ref, s2_ref, qT_ref, k_ref, vT_ref, kmean_ref, xb_ref):
    i, j = pl.program_id(1), pl.program_id(2)
    rows = x_ref.shape[1]
    nblk = rows // ATTN_TILE
    half = PARTIAL_ROT_DIM // 2

    @pl.when(j == 0)
    def _():
        xb_ref[...] = x_ref[0].astype(MXU_DTYPE)

    hm = _dot(xb_ref[...], w_ref[0])

    def rope(h):
        xh = hm[:, h * HEAD_DIM:(h + 1) * HEAD_DIM]
        return (xh * cf_ref[0] + pltpu.roll(xh, half, 1) * s1_ref[0]
                + pltpu.roll(xh, LANES - half, 1) * s2_ref[0])

    @pl.when(j == 0)
    def _():
        for h in range(MOBA_HEADS):
            q = rope(h)
            for blk in range(nblk):
                qT_ref[0, h, blk] = q[blk * ATTN_TILE:(blk + 1) * ATTN_TILE].T.astype(MXU_DTYPE)

    @pl.when(j == 1)
    def _():
        for h in range(MOBA_HEADS):
            k = rope(h)
            k_ref[0, h] = k.astype(MXU_DTYPE)
            for blk in range(nblk):
                mean = jnp.mean(k[blk * MOBA_BLOCK:(blk + 1) * MOBA_BLOCK], axis=0, keepdims=True)
                kmean_ref[0, h, pl.ds(i * nblk + blk, 1), :] = mean

    @pl.when(j == 2)
    def _():
        for h in range(MOBA_HEADS):
            v = hm[:, h * HEAD_DIM:(h + 1) * HEAD_DIM]
            for blk in range(nblk):
                vT_ref[0, h, blk] = v[blk * ATTN_TILE:(blk + 1) * ATTN_TILE].T.astype(MXU_DTYPE)


def _moba_prep(x, w_m, cf, s1, s2):
    B, S, D = x.shape
    rows = min(MOBA_PREP_ROWS, S)
    nb = S // ATTN_TILE
    H = MOBA_HEADS
    tspec = pl.BlockSpec((1, rows, LANES), lambda b, i, j: (b, i, 0))
    tile_t = pl.BlockSpec((1, H, rows // ATTN_TILE, HEAD_DIM, ATTN_TILE), lambda b, i, j: (b, 0, i, 0, 0))
    shape_t = jax.ShapeDtypeStruct((B, H, nb, HEAD_DIM, ATTN_TILE), MXU_DTYPE)
    return pl.pallas_call(
        _moba_prep_kernel,
        grid=(B, S // rows, 3),
        in_specs=[pl.BlockSpec((1, rows, D), lambda b, i, j: (b, i, 0)),
                  pl.BlockSpec((1, D, H * HEAD_DIM), lambda b, i, j: (j, 0, 0)),
                  tspec, tspec, tspec],
        out_specs=[tile_t,
                   pl.BlockSpec((1, H, rows, HEAD_DIM), lambda b, i, j: (b, 0, i, 0)),
                   tile_t,
                   pl.BlockSpec((1, H, nb, HEAD_DIM), lambda b, i, j: (b, 0, 0, 0))],
        out_shape=[shape_t,
                   jax.ShapeDtypeStruct((B, H, S, HEAD_DIM), MXU_DTYPE),
                   shape_t,
                   jax.ShapeDtypeStruct((B, H, nb, HEAD_DIM), jnp.float32)],
        scratch_shapes=[pltpu.VMEM((rows, D), MXU_DTYPE)],
        compiler_params=_params("parallel", "arbitrary", "arbitrary"),
        name="moba_prep",
    )(x, w_m, cf, s1, s2)


def _attn_kernel(*refs, scale, gated):
    if gated:
        qT_ref, k_ref, vT_ref, kmean_ref, o_ref, bias_ref = refs
    else:
        qT_ref, k_ref, vT_ref, o_ref = refs
    i = pl.program_id(2)
    T = ATTN_TILE
    qT = qT_ref[0, 0, 0]
    c = scale * LOG2E
    neg_inf = -jnp.inf

    if gated:
        nb = kmean_ref.shape[2]
        gate = _dot(kmean_ref[0, 0].astype(MXU_DTYPE), qT)
        row = lax.broadcasted_iota(jnp.int32, gate.shape, 0)
        past = row < i
        gate = jnp.where(past, gate, neg_inf)
        rank = jnp.zeros(gate.shape, jnp.int32)
        for jp in range(nb):
            gj = gate[jp:jp + 1, :]
            beats = (gj > gate) | ((gj == gate) & (jp < row))
            rank = rank + beats.astype(jnp.int32)
        sel = past & (rank < MOBA_TOPK)
        bias_ref[...] = jnp.where(sel, 0.0, neg_inf)

    def scores(j):
        kb = k_ref[0, 0, pl.ds(pl.multiple_of(j * T, T), T), :]
        return _dot(kb, qT) * c

    s = scores(i)
    kpos = lax.broadcasted_iota(jnp.int32, (T, T), 0)
    qpos = lax.broadcasted_iota(jnp.int32, (T, T), 1)
    s = jnp.where(kpos <= qpos, s, neg_inf)
    m = jnp.max(s, axis=0, keepdims=True)
    p = jnp.exp2(s - m)
    l = jnp.sum(p, axis=0, keepdims=True)
    acc = _dot(vT_ref[0, 0, i], p.astype(MXU_DTYPE))

    def body(j, carry):
        m, l, acc = carry
        s = scores(j)
        if gated:
            s = s + bias_ref[pl.ds(j, 1), :]
        m_new = jnp.maximum(m, jnp.max(s, axis=0, keepdims=True))
        alpha = jnp.exp2(m - m_new)
        p = jnp.exp2(s - m_new)
        l = alpha * l + jnp.sum(p, axis=0, keepdims=True)
        acc = alpha * acc + _dot(vT_ref[0, 0, j], p.astype(MXU_DTYPE))
        return m_new, l, acc

    m, l, acc = lax.fori_loop(0, i, body, (m, l, acc))
    o_ref[0] = (acc / l).T.astype(o_ref.dtype)


def _attention(qT, k, vT, kmean, *, scale):
    B, H, nb, d, T = qT.shape
    S = nb * T
    dv = vT.shape[3]
    gated = kmean is not None
    in_specs = [pl.BlockSpec((1, 1, 1, d, T), lambda b, h, i: (b, h, i, 0, 0)),
                pl.BlockSpec((1, 1, S, d), lambda b, h, i: (b, h, 0, 0)),
                pl.BlockSpec((1, 1, nb, dv, T), lambda b, h, i: (b, h, 0, 0, 0))]
    args = [qT, k, vT]
    scratch = []
    if gated:
        in_specs.append(pl.BlockSpec((1, 1, nb, d), lambda b, h, i: (b, h, 0, 0)))
        args.append(kmean)
        scratch.append(pltpu.VMEM((nb, T), jnp.float32))
    return pl.pallas_call(
        functools.partial(_attn_kernel, scale=scale, gated=gated),
        grid=(B, H, nb),
        in_specs=in_specs,
        out_specs=pl.BlockSpec((1, T, dv), lambda b, h, i: (b, i, h)),
        out_shape=jax.ShapeDtypeStruct((B, S, H * dv), MXU_DTYPE),
        scratch_shapes=scratch,
        compiler_params=_params("parallel", "parallel", "parallel"),
        name="moba_attention" if gated else "mla_attention",
    )(*args)


def _layer_norm(y, g, b):
    mu = jnp.mean(y, axis=-1, keepdims=True)
    yc = y - mu
    var = jnp.mean(jnp.square(yc), axis=-1, keepdims=True)
    return yc * lax.rsqrt(var + LN_EPS) * g + b


def _outproj_kernel(a1_ref, a2_ref, w1_ref, w2_ref, x_ref, g_ref, b_ref, o_ref):
    mix = _dot(a1_ref[...], w1_ref[...]) + _dot(a2_ref[...], w2_ref[...])
    o_ref[...] = _layer_norm(DEEPNORM_ALPHA * x_ref[...] + mix, g_ref[...], b_ref[...])


def _outproj_ln(a1, a2, w1, w2, x, g, b):
    M, D = x.shape
    rows = min(OUT_ROWS, M)
    K = a1.shape[1]
    full = lambda a: pl.BlockSpec(a.shape, lambda i: (0,) * a.ndim)
    return pl.pallas_call(
        _outproj_kernel,
        grid=(M // rows,),
        in_specs=[pl.BlockSpec((rows, K), lambda i: (i, 0)), pl.BlockSpec((rows, K), lambda i: (i, 0)),
                  full(w1), full(w2), pl.BlockSpec((rows, D), lambda i: (i, 0)), full(g), full(b)],
        out_specs=pl.BlockSpec((rows, D), lambda i: (i, 0)),
        out_shape=jax.ShapeDtypeStruct((M, D), jnp.float32),
        compiler_params=_params("parallel"),
        name="outproj_ln1",
    )(a1, a2, w1, w2, x, g, b)


def _ffn_kernel(x_ref, wup_ref, wdn_ref, g_ref, b_ref, o_ref, xb_ref, acc_ref):
    f = pl.program_id(1)

    @pl.when(f == 0)
    def _():
        xb_ref[...] = x_ref[...].astype(MXU_DTYPE)
        acc_ref[...] = jnp.zeros_like(acc_ref)

    u = jnp.maximum(_dot(xb_ref[...], wup_ref[...]), 0.0)
    acc_ref[...] += _dot(jnp.square(u).astype(MXU_DTYPE), wdn_ref[...])

    @pl.when(f == pl.num_programs(1) - 1)
    def _():
        o_ref[...] = _layer_norm(DEEPNORM_ALPHA * x_ref[...] + acc_ref[...], g_ref[...], b_ref[...])


def _ffn_ln(x, wup, wdn, g, b):
    M, D = x.shape
    F = wup.shape[1]
    rows = min(FFN_ROWS, M)
    cols = min(FFN_COLS, F)
    vec = pl.BlockSpec((1, D), lambda i, f: (0, 0))
    return pl.pallas_call(
        _ffn_kernel,
        grid=(M // rows, F // cols),
        in_specs=[pl.BlockSpec((rows, D), lambda i, f: (i, 0)),
                  pl.BlockSpec((D, cols), lambda i, f: (0, f)),
                  pl.BlockSpec((cols, D), lambda i, f: (f, 0)),
                  vec, vec],
        out_specs=pl.BlockSpec((rows, D), lambda i, f: (i, 0)),
        out_shape=jax.ShapeDtypeStruct((M, D), jnp.float32),
        scratch_shapes=[pltpu.VMEM((rows, D), MXU_DTYPE), pltpu.VMEM((rows, D), jnp.float32)],
        compiler_params=_params("parallel", "arbitrary"),
        name="ffn_ln2",
    )(x, wup, wdn, g, b)


def _layer_weights(w_in, w_uq, w_ukv, w_out, w_up, w_down):
    D = w_in.shape[0]
    half = MLA_ROPE_DIM // 2
    swap = np.concatenate([np.arange(half, MLA_ROPE_DIM), np.arange(half)])
    c0 = MLA_Q_RANK + MLA_KV_RANK
    c1 = c0 + MLA_ROPE_DIM
    mw = MOBA_HEADS * HEAD_DIM
    w_lat = jnp.concatenate([w_in[:, :c1], w_in[:, c0:c1][:, swap]], axis=1)
    w_m = jnp.stack([w_in[:, c1:c1 + mw], w_in[:, c1 + mw:c1 + 2 * mw], w_in[:, c1 + 2 * mw:c1 + 3 * mw]])
    uq = w_uq.reshape(MLA_Q_RANK, MLA_HEADS, HEAD_DIM + MLA_ROPE_DIM)
    rope_cols = uq[:, :, HEAD_DIM:]
    uq = jnp.concatenate([uq, rope_cols[:, :, swap]], axis=-1).reshape(MLA_Q_RANK, MLA_HEADS * 2 * HEAD_DIM)
    ukv = w_ukv.reshape(MLA_KV_RANK, MLA_HEADS, 2, HEAD_DIM).transpose(0, 2, 1, 3).reshape(MLA_KV_RANK, -1)
    n1 = MLA_HEADS * HEAD_DIM
    cast = lambda w: w.astype(MXU_DTYPE)
    return dict(w_lat=cast(w_lat), w_m=cast(w_m), uq=cast(uq), ukv=cast(ukv),
                wo1=cast(w_out[:n1]), wo2=cast(w_out[n1:]), wup=cast(w_up), wdn=cast(w_down))


def kernel(x, positions, w_in, mla_q_norm, mla_kv_norm, w_uq, w_ukv, w_out, ln1_g, ln1_b, w_up, w_down, ln2_g, ln2_b):
    B, S, D = x.shape
    assert S % ATTN_TILE == 0 and w_in.shape[0] == DEPTH
    tmla, cf, s1, s2 = _rope_tables(positions)
    for l in range(DEPTH):
        w = _layer_weights(w_in[l], w_uq[l], w_ukv[l], w_out[l], w_up[l], w_down[l])
        row = lambda v: v[l].reshape(1, -1)
        qT_a, k_a, vT_a = _mla_prep(x, w["w_lat"], row(mla_q_norm), row(mla_kv_norm), w["uq"], w["ukv"], tmla)
        qT_b, k_b, vT_b, kmean = _moba_prep(x, w["w_m"], cf, s1, s2)
        out_a = _attention(qT_a, k_a, vT_a, None, scale=MLA_SCALE)
        out_b = _attention(qT_b, k_b, vT_b, kmean, scale=MOBA_SCALE)
        x1 = _outproj_ln(out_a.reshape(B * S, -1), out_b.reshape(B * S, -1), w["wo1"], w["wo2"],
                         x.reshape(B * S, D), row(ln1_g), row(ln1_b))
        x = _ffn_ln(x1, w["wup"], w["wdn"], row(ln2_g), row(ln2_b)).reshape(B, S, D)
    return x
```

```python
import functools
import math

import numpy as np
import jax
import jax.numpy as jnp
from jax import lax
from jax.experimental import pallas as pl
from jax.experimental.pallas import tpu as pltpu

HEAD_DIM = 128
MLA_HEADS = 8
MOBA_HEADS = 8
MLA_Q_RANK = 384
MLA_KV_RANK = 256
MLA_ROPE_DIM = 64
MOBA_BLOCK = 256
MOBA_TOPK = 3
ROPE_THETA = 500000.0
PARTIAL_ROT_DIM = 32
LN_EPS = 1e-5
RMS_EPS = 1e-6
DEPTH = 1
DEEPNORM_ALPHA = (2 * DEPTH) ** 0.25
MLA_SCALE = 1.0 / math.sqrt(HEAD_DIM + MLA_ROPE_DIM)
MOBA_SCALE = 1.0 / math.sqrt(HEAD_DIM)
LOG2E = math.log2(math.e)

LANES = 128
MXU_DTYPE = jnp.bfloat16
VMEM_LIMIT_BYTES = 56 * 1024 * 1024

ATTN_TILE = MOBA_BLOCK
ATTN_HEADS = 4
MLA_PREP_ROWS = 512
MOBA_PREP_ROWS = 1024
OUT_ROWS = 512
FFN_ROWS = 512
FFN_COLS = 1024
TABLE_ROWS = 1024


def _dot(a, b):
    return jnp.dot(a, b, preferred_element_type=jnp.float32)


def _params(*sem):
    return pltpu.CompilerParams(dimension_semantics=sem, vmem_limit_bytes=VMEM_LIMIT_BYTES)


def _tables_kernel(pos_ref, invf_ref, tmla_ref, cf_ref, s1_ref, s2_ref):
    pos = pos_ref[...].astype(jnp.float32)
    lane = lax.broadcasted_iota(jnp.int32, tmla_ref.shape, 1)
    ang = pos * invf_ref[0:1, :]
    c, s = jnp.cos(ang), jnp.sin(ang)
    tmla_ref[...] = jnp.where(lane < 64, c, jnp.where(lane < 96, -s, s))
    ang2 = pos * invf_ref[1:2, :]
    c2, s2 = jnp.cos(ang2), jnp.sin(ang2)
    half = PARTIAL_ROT_DIM // 2
    cf_ref[...] = jnp.where(lane < PARTIAL_ROT_DIM, c2, 1.0)
    s1_ref[...] = jnp.where((lane >= half) & (lane < PARTIAL_ROT_DIM), s2, 0.0)
    s2_ref[...] = jnp.where(lane < half, -s2, 0.0)


def _rope_tables(positions):
    B, S = positions.shape
    M = B * S
    rows = min(TABLE_ROWS, M)
    f32 = jnp.float32
    half_a = MLA_ROPE_DIM // 2
    invf_a = ROPE_THETA ** (-jnp.arange(half_a, dtype=f32) * (2.0 / MLA_ROPE_DIM))
    half_b = PARTIAL_ROT_DIM // 2
    invf_b = ROPE_THETA ** (-jnp.arange(half_b, dtype=f32) * (2.0 / PARTIAL_ROT_DIM))
    row_a = jnp.tile(invf_a, LANES // half_a)
    row_b = jnp.concatenate([invf_b, invf_b, jnp.zeros((LANES - PARTIAL_ROT_DIM,), f32)])
    invf = jnp.stack([row_a, row_b])
    out = jax.ShapeDtypeStruct((M, LANES), f32)
    spec = pl.BlockSpec((rows, LANES), lambda i: (i, 0))
    tabs = pl.pallas_call(
        _tables_kernel,
        grid=(M // rows,),
        in_specs=[pl.BlockSpec((rows, 1), lambda i: (i, 0)), pl.BlockSpec((2, LANES), lambda i: (0, 0))],
        out_specs=[spec] * 4,
        out_shape=[out] * 4,
        compiler_params=_params("parallel"),
        name="rope_tables",
    )(positions.reshape(M, 1), invf)
    return [t.reshape(B, S, LANES) for t in tabs]


def _rms(x, g):
    y = x * lax.rsqrt(jnp.mean(jnp.square(x), axis=-1, keepdims=True) + RMS_EPS)
    return y * g


def _mla_prep_kernel(x_ref, wlat_ref, gq_ref, gkv_ref, wuq_ref, wukv_ref, t_ref, qT_ref, k_ref, vT_ref):
    rows = x_ref.shape[1]
    nblk = rows // ATTN_TILE
    xb = x_ref[0].astype(MXU_DTYPE)
    lat = _dot(xb, wlat_ref[...])
    cq = lat[:, :MLA_Q_RANK]
    ckv = lat[:, MLA_Q_RANK:MLA_Q_RANK + MLA_KV_RANK]
    kr2 = lat[:, MLA_Q_RANK + MLA_KV_RANK:]
    tab = t_ref[0]
    lane = lax.broadcasted_iota(jnp.int32, tab.shape, 1)
    t = kr2 * tab
    k_rope = jnp.where(lane < MLA_ROPE_DIM, t + pltpu.roll(t, MLA_ROPE_DIM, 1), 0.0).astype(MXU_DTYPE)
    qall = _dot(_rms(cq, gq_ref[...]).astype(MXU_DTYPE), wuq_ref[...])
    kvall = _dot(_rms(ckv, gkv_ref[...]).astype(MXU_DTYPE), wukv_ref[...])
    dq = 2 * HEAD_DIM
    for h in range(MLA_HEADS):
        nope = qall[:, h * dq:h * dq + HEAD_DIM]
        t = qall[:, h * dq + HEAD_DIM:(h + 1) * dq] * tab
        rope = t + pltpu.roll(t, MLA_ROPE_DIM, 1)
        for blk in range(nblk):
            r = slice(blk * ATTN_TILE, (blk + 1) * ATTN_TILE)
            qT_ref[0, h, blk, 0:HEAD_DIM, :] = nope[r].T.astype(MXU_DTYPE)
            qT_ref[0, h, blk, HEAD_DIM:dq, :] = rope[r].T.astype(MXU_DTYPE)
        k_ref[0, h, :, 0:HEAD_DIM] = kvall[:, h * HEAD_DIM:(h + 1) * HEAD_DIM].astype(MXU_DTYPE)
        k_ref[0, h, :, HEAD_DIM:dq] = k_rope
        v = kvall[:, (MLA_HEADS + h) * HEAD_DIM:(MLA_HEADS + h + 1) * HEAD_DIM]
        for blk in range(nblk):
            r = slice(blk * ATTN_TILE, (blk + 1) * ATTN_TILE)
            vT_ref[0, h, blk] = v[r].T.astype(MXU_DTYPE)


def _mla_prep(x, w_lat, gq, gkv, wuq, wukv, tmla):
    B, S, D = x.shape
    rows = min(MLA_PREP_ROWS, S)
    nb = S // ATTN_TILE
    H, dq = MLA_HEADS, 2 * HEAD_DIM
    full = lambda a: pl.BlockSpec(a.shape, lambda b, i: (0,) * a.ndim)
    return pl.pallas_call(
        _mla_prep_kernel,
        grid=(B, S // rows),
        in_specs=[pl.BlockSpec((1, rows, D), lambda b, i: (b, i, 0)),
                  full(w_lat), full(gq), full(gkv), full(wuq), full(wukv),
                  pl.BlockSpec((1, rows, LANES), lambda b, i: (b, i, 0))],
        out_specs=[pl.BlockSpec((1, H, rows // ATTN_TILE, dq, ATTN_TILE), lambda b, i: (b, 0, i, 0, 0)),
                   pl.BlockSpec((1, H, rows, dq), lambda b, i: (b, 0, i, 0)),
                   pl.BlockSpec((1, H, rows // ATTN_TILE, HEAD_DIM, ATTN_TILE), lambda b, i: (b, 0, i, 0, 0))],
        out_shape=[jax.ShapeDtypeStruct((B, H, nb, dq, ATTN_TILE), MXU_DTYPE),
                   jax.ShapeDtypeStruct((B, H, S, dq), MXU_DTYPE),
                   jax.ShapeDtypeStruct((B, H, nb, HEAD_DIM, ATTN_TILE), MXU_DTYPE)],
        compiler_params=_params("parallel", "parallel"),
        name="mla_prep",
    )(x, w_lat, gq, gkv, wuq, wukv, tmla)


def _moba_prep_kernel(x_ref, w_ref, cf_ref, s1_ref, s2_ref, qT_ref, k_ref, vT_ref, kmean_ref, xb_ref):
    i, j = pl.program_id(1), pl.program_id(2)
    rows = x_ref.shape[1]
    nblk = rows // ATTN_TILE
    half = PARTIAL_ROT_DIM // 2

    @pl.when(j == 0)
    def _():
        xb_ref[...] = x_ref[0].astype(MXU_DTYPE)

    hm = _dot(xb_ref[...], w_ref[0])

    def rope(h):
        xh = hm[:, h * HEAD_DIM:(h + 1) * HEAD_DIM]
        return (xh * cf_ref[0] + pltpu.roll(xh, half, 1) * s1_ref[0]
                + pltpu.roll(xh, LANES - half, 1) * s2_ref[0])

    @pl.when(j == 0)
    def _():
        for h in range(MOBA_HEADS):
            q = rope(h)
            for blk in range(nblk):
                qT_ref[0, h, blk] = q[blk * ATTN_TILE:(blk + 1) * ATTN_TILE].T.astype(MXU_DTYPE)

    @pl.when(j == 1)
    def _():
        for h in range(MOBA_HEADS):
            k = rope(h)
            k_ref[0, h] = k.astype(MXU_DTYPE)
            for blk in range(nblk):
                mean = jnp.mean(k[blk * MOBA_BLOCK:(blk + 1) * MOBA_BLOCK], axis=0, keepdims=True)
                kmean_ref[0, h, pl.ds(i * nblk + blk, 1), :] = mean

    @pl.when(j == 2)
    def _():
        for h in range(MOBA_HEADS):
            v = hm[:, h * HEAD_DIM:(h + 1) * HEAD_DIM]
            for blk in range(nblk):
                vT_ref[0, h, blk] = v[blk * ATTN_TILE:(blk + 1) * ATTN_TILE].T.astype(MXU_DTYPE)


def _moba_prep(x, w_m, cf, s1, s2):
    B, S, D = x.shape
    rows = min(MOBA_PREP_ROWS, S)
    nb = S // ATTN_TILE
    H = MOBA_HEADS
    tspec = pl.BlockSpec((1, rows, LANES), lambda b, i, j: (b, i, 0))
    tile_t = pl.BlockSpec((1, H, rows // ATTN_TILE, HEAD_DIM, ATTN_TILE), lambda b, i, j: (b, 0, i, 0, 0))
    shape_t = jax.ShapeDtypeStruct((B, H, nb, HEAD_DIM, ATTN_TILE), MXU_DTYPE)
    return pl.pallas_call(
        _moba_prep_kernel,
        grid=(B, S // rows, 3),
        in_specs=[pl.BlockSpec((1, rows, D), lambda b, i, j: (b, i, 0)),
                  pl.BlockSpec((1, D, H * HEAD_DIM), lambda b, i, j: (j, 0, 0)),
                  tspec, tspec, tspec],
        out_specs=[tile_t,
                   pl.BlockSpec((1, H, rows, HEAD_DIM), lambda b, i, j: (b, 0, i, 0)),
                   tile_t,
                   pl.BlockSpec((1, H, nb, HEAD_DIM), lambda b, i, j: (b, 0, 0, 0))],
        out_shape=[shape_t,
                   jax.ShapeDtypeStruct((B, H, S, HEAD_DIM), MXU_DTYPE),
                   shape_t,
                   jax.ShapeDtypeStruct((B, H, nb, HEAD_DIM), jnp.float32)],
        scratch_shapes=[pltpu.VMEM((rows, D), MXU_DTYPE)],
        compiler_params=_params("parallel", "arbitrary", "arbitrary"),
        name="moba_prep",
    )(x, w_m, cf, s1, s2)


def _attn_kernel(*refs, scale, gated):
    if gated:
        qT_ref, k_ref, vT_ref, kmean_ref, o_ref, acc_ref, s_ref, p_ref, bias_ref = refs
    else:
        qT_ref, k_ref, vT_ref, o_ref, acc_ref, s_ref, p_ref = refs
    i = pl.program_id(2)
    T = ATTN_TILE
    G = qT_ref.shape[1]
    nb, dv = vT_ref.shape[2], vT_ref.shape[3]
    c = scale * LOG2E
    neg_inf = -jnp.inf

    if gated:
        nb = kmean_ref.shape[2]
        gates = [_dot(kmean_ref[0, g].astype(MXU_DTYPE), qT_ref[0, g, 0]) for g in range(G)]
        for g in range(G):
            gate = gates[g]
            row = lax.broadcasted_iota(jnp.int32, gate.shape, 0)
            past = row < i
            gate = jnp.where(past, gate, neg_inf)
            rank = jnp.zeros(gate.shape, jnp.int32)
            for jp in range(nb):
                gj = gate[jp:jp + 1, :]
                beats = (gj > gate) | ((gj == gate) & (jp < row))
                rank = rank + beats.astype(jnp.int32)
            sel = past & (rank < MOBA_TOPK)
            bias_ref[g] = jnp.where(sel, 0.0, neg_inf)

    heads = range(G)

    def scores(g, j):
        kb = k_ref[0, g, pl.ds(pl.multiple_of(j * T, T), T), :]
        return _dot(kb, qT_ref[0, g, 0]) * c

    def pv(g, j, p):
        return _dot(vT_ref[0, g, j], p)

    s_own = [scores(g, i) for g in heads]
    for g in heads:
        s_ref[0, g] = scores(g, 0)
    kpos = lax.broadcasted_iota(jnp.int32, (T, T), 0)
    qpos = lax.broadcasted_iota(jnp.int32, (T, T), 1)
    causal = kpos <= qpos
    ms, ls = [], []
    for g in heads:
        s = jnp.where(causal, s_own[g], neg_inf)
        m = jnp.max(s, axis=0, keepdims=True)
        p = jnp.exp2(s - m)
        ms.append(m)
        ls.append(jnp.sum(p, axis=0, keepdims=True))
        p_ref[g] = p.astype(MXU_DTYPE)
        acc_ref[g] = jnp.zeros((dv, T), jnp.float32)
    ones = tuple(jnp.ones((1, T), jnp.float32) for _ in heads)

    def stage(t, slot, carry):
        a_pend, ms, ls, j_pend = carry
        pvs = [pv(g, j_pend, p_ref[g]) for g in heads]
        nxt = jnp.minimum(t + 1, nb - 1)
        for g in heads:
            s_ref[1 - slot, g] = scores(g, nxt)
        new_ms, new_ls, new_as = [], [], []
        for g in heads:
            cm = jnp.max(s_ref[slot, g], axis=0, keepdims=True)
            if gated:
                b = bias_ref[g, pl.ds(t, 1), :]
            else:
                b = jnp.where(t < i, 0.0, neg_inf)
            m_new = jnp.maximum(ms[g], cm + b)
            alpha = jnp.exp2(ms[g] - m_new)
            p = jnp.exp2(s_ref[slot, g] - (m_new - b))
            new_ms.append(m_new)
            new_ls.append(alpha * ls[g] + jnp.sum(p, axis=0, keepdims=True))
            new_as.append(alpha)
            acc_ref[g] = a_pend[g] * acc_ref[g] + pvs[g]
            p_ref[g] = p.astype(MXU_DTYPE)
        return tuple(new_as), tuple(new_ms), tuple(new_ls), t

    def body(u, carry):
        return stage(2 * u + 1, 1, stage(2 * u, 0, carry))

    carry = (ones, tuple(ms), tuple(ls), i)
    a_pend, _, ls, j_pend = lax.fori_loop(0, (i + 1) // 2, body, carry)
    for g in heads:
        acc = a_pend[g] * acc_ref[g] + pv(g, j_pend, p_ref[g])
        o_ref[0, :, g * dv:(g + 1) * dv] = (acc / ls[g]).T.astype(o_ref.dtype)


def _attention(qT, k, vT, kmean, *, scale):
    B, H, nb, d, T = qT.shape
    S = nb * T
    dv = vT.shape[3]
    G = ATTN_HEADS
    gated = kmean is not None
    in_specs = [pl.BlockSpec((1, G, 1, d, T), lambda b, h, i: (b, h, i, 0, 0)),
                pl.BlockSpec((1, G, S, d), lambda b, h, i: (b, h, 0, 0)),
                pl.BlockSpec((1, G, nb, dv, T), lambda b, h, i: (b, h, 0, 0, 0))]
    args = [qT, k, vT]
    scratch = [pltpu.VMEM((G, dv, T), jnp.float32),
               pltpu.VMEM((2, G, T, T), jnp.float32),
               pltpu.VMEM((G, T, T), MXU_DTYPE)]
    if gated:
        in_specs.append(pl.BlockSpec((1, G, nb, d), lambda b, h, i: (b, h, 0, 0)))
        args.append(kmean)
        scratch.append(pltpu.VMEM((G, nb, T), jnp.float32))
    return pl.pallas_call(
        functools.partial(_attn_kernel, scale=scale, gated=gated),
        grid=(B, H // G, nb),
        in_specs=in_specs,
        out_specs=pl.BlockSpec((1, T, G * dv), lambda b, h, i: (b, i, h)),
        out_shape=jax.ShapeDtypeStruct((B, S, H * dv), MXU_DTYPE),
        scratch_shapes=scratch,
        compiler_params=_params("parallel", "parallel", "parallel"),
        name="moba_attention" if gated else "mla_attention",
    )(*args)


def _layer_norm(y, g, b):
    mu = jnp.mean(y, axis=-1, keepdims=True)
    yc = y - mu
    var = jnp.mean(jnp.square(yc), axis=-1, keepdims=True)
    return yc * lax.rsqrt(var + LN_EPS) * g + b


def _outproj_kernel(a1_ref, a2_ref, w1_ref, w2_ref, x_ref, g_ref, b_ref, o_ref):
    mix = _dot(a1_ref[...], w1_ref[...]) + _dot(a2_ref[...], w2_ref[...])
    o_ref[...] = _layer_norm(DEEPNORM_ALPHA * x_ref[...] + mix, g_ref[...], b_ref[...])


def _outproj_ln(a1, a2, w1, w2, x, g, b):
    M, D = x.shape
    rows = min(OUT_ROWS, M)
    K = a1.shape[1]
    full = lambda a: pl.BlockSpec(a.shape, lambda i: (0,) * a.ndim)
    return pl.pallas_call(
        _outproj_kernel,
        grid=(M // rows,),
        in_specs=[pl.BlockSpec((rows, K), lambda i: (i, 0)), pl.BlockSpec((rows, K), lambda i: (i, 0)),
                  full(w1), full(w2), pl.BlockSpec((rows, D), lambda i: (i, 0)), full(g), full(b)],
        out_specs=pl.BlockSpec((rows, D), lambda i: (i, 0)),
        out_shape=jax.ShapeDtypeStruct((M, D), jnp.float32),
        compiler_params=_params("parallel"),
        name="outproj_ln1",
    )(a1, a2, w1, w2, x, g, b)


def _ffn_kernel(x_ref, wup_ref, wdn_ref, g_ref, b_ref, o_ref, xb_ref, acc_ref):
    f = pl.program_id(1)

    @pl.when(f == 0)
    def _():
        xb_ref[...] = x_ref[...].astype(MXU_DTYPE)
        acc_ref[...] = jnp.zeros_like(acc_ref)

    u = jnp.maximum(_dot(xb_ref[...], wup_ref[...]), 0.0)
    acc_ref[...] += _dot(jnp.square(u).astype(MXU_DTYPE), wdn_ref[...])

    @pl.when(f == pl.num_programs(1) - 1)
    def _():
        o_ref[...] = _layer_norm(DEEPNORM_ALPHA * x_ref[...] + acc_ref[...], g_ref[...], b_ref[...])


def _ffn_ln(x, wup, wdn, g, b):
    M, D = x.shape
    F = wup.shape[1]
    rows = min(FFN_ROWS, M)
    cols = min(FFN_COLS, F)
    vec = pl.BlockSpec((1, D), lambda i, f: (0, 0))
    return pl.pallas_call(
        _ffn_kernel,
        grid=(M // rows, F // cols),
        in_specs=[pl.BlockSpec((rows, D), lambda i, f: (i, 0)),
                  pl.BlockSpec((D, cols), lambda i, f: (0, f)),
                  pl.BlockSpec((cols, D), lambda i, f: (f, 0)),
                  vec, vec],
        out_specs=pl.BlockSpec((rows, D), lambda i, f: (i, 0)),
        out_shape=jax.ShapeDtypeStruct((M, D), jnp.float32),
        scratch_shapes=[pltpu.VMEM((rows, D), MXU_DTYPE), pltpu.VMEM((rows, D), jnp.float32)],
        compiler_params=_params("parallel", "arbitrary"),
        name="ffn_ln2",
    )(x, wup, wdn, g, b)


def _layer_weights(w_in, w_uq, w_ukv, w_out, w_up, w_down):
    D = w_in.shape[0]
    half = MLA_ROPE_DIM // 2
    swap = np.concatenate([np.arange(half, MLA_ROPE_DIM), np.arange(half)])
    c0 = MLA_Q_RANK + MLA_KV_RANK
    c1 = c0 + MLA_ROPE_DIM
    mw = MOBA_HEADS * HEAD_DIM
    w_lat = jnp.concatenate([w_in[:, :c1], w_in[:, c0:c1][:, swap]], axis=1)
    w_m = jnp.stack([w_in[:, c1:c1 + mw], w_in[:, c1 + mw:c1 + 2 * mw], w_in[:, c1 + 2 * mw:c1 + 3 * mw]])
    uq = w_uq.reshape(MLA_Q_RANK, MLA_HEADS, HEAD_DIM + MLA_ROPE_DIM)
    rope_cols = uq[:, :, HEAD_DIM:]
    uq = jnp.concatenate([uq, rope_cols[:, :, swap]], axis=-1).reshape(MLA_Q_RANK, MLA_HEADS * 2 * HEAD_DIM)
    ukv = w_ukv.reshape(MLA_KV_RANK, MLA_HEADS, 2, HEAD_DIM).transpose(0, 2, 1, 3).reshape(MLA_KV_RANK, -1)
    n1 = MLA_HEADS * HEAD_DIM
    cast = lambda w: w.astype(MXU_DTYPE)
    return dict(w_lat=cast(w_lat), w_m=cast(w_m), uq=cast(uq), ukv=cast(ukv),
                wo1=cast(w_out[:n1]), wo2=cast(w_out[n1:]), wup=cast(w_up), wdn=cast(w_down))


def kernel(x, positions, w_in, mla_q_norm, mla_kv_norm, w_uq, w_ukv, w_out, ln1_g, ln1_b, w_up, w_down, ln2_g, ln2_b):
    B, S, D = x.shape
    assert S % ATTN_TILE == 0 and w_in.shape[0] == DEPTH
    tmla, cf, s1, s2 = _rope_tables(positions)
    for l in range(DEPTH):
        w = _layer_weights(w_in[l], w_uq[l], w_ukv[l], w_out[l], w_up[l], w_down[l])
        row = lambda v: v[l].reshape(1, -1)
        qT_a, k_a, vT_a = _mla_prep(x, w["w_lat"], row(mla_q_norm), row(mla_kv_norm), w["uq"], w["ukv"], tmla)
        qT_b, k_b, vT_b, kmean = _moba_prep(x, w["w_m"], cf, s1, s2)
        out_a = _attention(qT_a, k_a, vT_a, None, scale=MLA_SCALE)
        out_b = _attention(qT_b, k_b, vT_b, kmean, scale=MOBA_SCALE)
        x1 = _outproj_ln(out_a.reshape(B * S, -1), out_b.reshape(B * S, -1), w["wo1"], w["wo2"],
                         x.reshape(B * S, D), row(ln1_g), row(ln1_b))
        x = _ffn_ln(x1, w["wup"], w["wdn"], row(ln2_g), row(ln2_b)).reshape(B, S, D)
    return x
```

```python
import functools
import math

import numpy as np
import jax
import jax.numpy as jnp
from jax import lax
from jax.experimental import pallas as pl
from jax.experimental.pallas import tpu as pltpu

HEAD_DIM = 128
MLA_HEADS = 8
MOBA_HEADS = 8
MLA_Q_RANK = 384
MLA_KV_RANK = 256
MLA_ROPE_DIM = 64
MOBA_BLOCK = 256
MOBA_TOPK = 3
ROPE_THETA = 500000.0
PARTIAL_ROT_DIM = 32
LN_EPS = 1e-5
RMS_EPS = 1e-6
DEPTH = 1
DEEPNORM_ALPHA = (2 * DEPTH) ** 0.25
MLA_SCALE = 1.0 / math.sqrt(HEAD_DIM + MLA_ROPE_DIM)
MOBA_SCALE = 1.0 / math.sqrt(HEAD_DIM)
LOG2E = math.log2(math.e)

LANES = 128
MXU_DTYPE = jnp.bfloat16
VMEM_LIMIT_BYTES = 56 * 1024 * 1024

ATTN_TILE = MOBA_BLOCK
ATTN_HEADS = 4
BF16_SUBLANES = 16
VT_ROWS = HEAD_DIM + BF16_SUBLANES
MLA_PREP_ROWS = 512
MOBA_PREP_ROWS = 1024
OUT_ROWS = 512
FFN_ROWS = 512
FFN_COLS = 1024
TABLE_ROWS = 1024


def _dot(a, b):
    return jnp.dot(a, b, preferred_element_type=jnp.float32)


def _params(*sem):
    return pltpu.CompilerParams(dimension_semantics=sem, vmem_limit_bytes=VMEM_LIMIT_BYTES)


def _tables_kernel(pos_ref, invf_ref, tmla_ref, cf_ref, s1_ref, s2_ref):
    pos = pos_ref[...].astype(jnp.float32)
    lane = lax.broadcasted_iota(jnp.int32, tmla_ref.shape, 1)
    ang = pos * invf_ref[0:1, :]
    c, s = jnp.cos(ang), jnp.sin(ang)
    tmla_ref[...] = jnp.where(lane < 64, c, jnp.where(lane < 96, -s, s))
    ang2 = pos * invf_ref[1:2, :]
    c2, s2 = jnp.cos(ang2), jnp.sin(ang2)
    half = PARTIAL_ROT_DIM // 2
    cf_ref[...] = jnp.where(lane < PARTIAL_ROT_DIM, c2, 1.0)
    s1_ref[...] = jnp.where((lane >= half) & (lane < PARTIAL_ROT_DIM), s2, 0.0)
    s2_ref[...] = jnp.where(lane < half, -s2, 0.0)


def _rope_tables(positions):
    B, S = positions.shape
    M = B * S
    rows = min(TABLE_ROWS, M)
    f32 = jnp.float32
    half_a = MLA_ROPE_DIM // 2
    invf_a = ROPE_THETA ** (-jnp.arange(half_a, dtype=f32) * (2.0 / MLA_ROPE_DIM))
    half_b = PARTIAL_ROT_DIM // 2
    invf_b = ROPE_THETA ** (-jnp.arange(half_b, dtype=f32) * (2.0 / PARTIAL_ROT_DIM))
    row_a = jnp.tile(invf_a, LANES // half_a)
    row_b = jnp.concatenate([invf_b, invf_b, jnp.zeros((LANES - PARTIAL_ROT_DIM,), f32)])
    invf = jnp.stack([row_a, row_b])
    out = jax.ShapeDtypeStruct((M, LANES), f32)
    spec = pl.BlockSpec((rows, LANES), lambda i: (i, 0))
    tabs = pl.pallas_call(
        _tables_kernel,
        grid=(M // rows,),
        in_specs=[pl.BlockSpec((rows, 1), lambda i: (i, 0)), pl.BlockSpec((2, LANES), lambda i: (0, 0))],
        out_specs=[spec] * 4,
        out_shape=[out] * 4,
        compiler_params=_params("parallel"),
        name="rope_tables",
    )(positions.reshape(M, 1), invf)
    return [t.reshape(B, S, LANES) for t in tabs]


def _rms(x, g):
    y = x * lax.rsqrt(jnp.mean(jnp.square(x), axis=-1, keepdims=True) + RMS_EPS)
    return y * g


def _store_vT(vT_ref, h, blk, v):
    vT_ref[0, h, blk, 0:HEAD_DIM, :] = v.T.astype(MXU_DTYPE)
    row = lax.broadcasted_iota(jnp.int32, (BF16_SUBLANES, ATTN_TILE), 0)
    vT_ref[0, h, blk, HEAD_DIM:VT_ROWS, :] = jnp.where(row == 0, 1.0, 0.0).astype(MXU_DTYPE)


def _mla_prep_kernel(x_ref, wlat_ref, gq_ref, gkv_ref, wuq_ref, wukv_ref, t_ref, qT_ref, k_ref, vT_ref):
    rows = x_ref.shape[1]
    nblk = rows // ATTN_TILE
    xb = x_ref[0].astype(MXU_DTYPE)
    lat = _dot(xb, wlat_ref[...])
    cq = lat[:, :MLA_Q_RANK]
    ckv = lat[:, MLA_Q_RANK:MLA_Q_RANK + MLA_KV_RANK]
    kr2 = lat[:, MLA_Q_RANK + MLA_KV_RANK:]
    tab = t_ref[0]
    lane = lax.broadcasted_iota(jnp.int32, tab.shape, 1)
    t = kr2 * tab
    k_rope = jnp.where(lane < MLA_ROPE_DIM, t + pltpu.roll(t, MLA_ROPE_DIM, 1), 0.0).astype(MXU_DTYPE)
    qall = _dot(_rms(cq, gq_ref[...]).astype(MXU_DTYPE), wuq_ref[...])
    kvall = _dot(_rms(ckv, gkv_ref[...]).astype(MXU_DTYPE), wukv_ref[...])
    dq = 2 * HEAD_DIM
    qall = qall * (MLA_SCALE * LOG2E)
    for h in range(MLA_HEADS):
        nope = qall[:, h * dq:h * dq + HEAD_DIM]
        t = qall[:, h * dq + HEAD_DIM:(h + 1) * dq] * tab
        rope = t + pltpu.roll(t, MLA_ROPE_DIM, 1)
        for blk in range(nblk):
            r = slice(blk * ATTN_TILE, (blk + 1) * ATTN_TILE)
            qT_ref[0, h, blk, 0:HEAD_DIM, :] = nope[r].T.astype(MXU_DTYPE)
            qT_ref[0, h, blk, HEAD_DIM:dq, :] = rope[r].T.astype(MXU_DTYPE)
        k_ref[0, h, :, 0:HEAD_DIM] = kvall[:, h * HEAD_DIM:(h + 1) * HEAD_DIM].astype(MXU_DTYPE)
        k_ref[0, h, :, HEAD_DIM:dq] = k_rope
        v = kvall[:, (MLA_HEADS + h) * HEAD_DIM:(MLA_HEADS + h + 1) * HEAD_DIM]
        for blk in range(nblk):
            _store_vT(vT_ref, h, blk, v[blk * ATTN_TILE:(blk + 1) * ATTN_TILE])


def _mla_prep(x, w_lat, gq, gkv, wuq, wukv, tmla):
    B, S, D = x.shape
    rows = min(MLA_PREP_ROWS, S)
    nb = S // ATTN_TILE
    H, dq = MLA_HEADS, 2 * HEAD_DIM
    full = lambda a: pl.BlockSpec(a.shape, lambda b, i: (0,) * a.ndim)
    return pl.pallas_call(
        _mla_prep_kernel,
        grid=(B, S // rows),
        in_specs=[pl.BlockSpec((1, rows, D), lambda b, i: (b, i, 0)),
                  full(w_lat), full(gq), full(gkv), full(wuq), full(wukv),
                  pl.BlockSpec((1, rows, LANES), lambda b, i: (b, i, 0))],
        out_specs=[pl.BlockSpec((1, H, rows // ATTN_TILE, dq, ATTN_TILE), lambda b, i: (b, 0, i, 0, 0)),
                   pl.BlockSpec((1, H, rows, dq), lambda b, i: (b, 0, i, 0)),
                   pl.BlockSpec((1, H, rows // ATTN_TILE, VT_ROWS, ATTN_TILE), lambda b, i: (b, 0, i, 0, 0))],
        out_shape=[jax.ShapeDtypeStruct((B, H, nb, dq, ATTN_TILE), MXU_DTYPE),
                   jax.ShapeDtypeStruct((B, H, S, dq), MXU_DTYPE),
                   jax.ShapeDtypeStruct((B, H, nb, VT_ROWS, ATTN_TILE), MXU_DTYPE)],
        compiler_params=_params("parallel", "parallel"),
        name="mla_prep",
    )(x, w_lat, gq, gkv, wuq, wukv, tmla)


def _moba_prep_kernel(x_ref, w_ref, cf_ref, s1_ref, s2_ref, qT_ref, k_ref, vT_ref, kmean_ref, xb_ref):
    i, j = pl.program_id(1), pl.program_id(2)
    rows = x_ref.shape[1]
    nblk = rows // ATTN_TILE
    half = PARTIAL_ROT_DIM // 2

    @pl.when(j == 0)
    def _():
        xb_ref[...] = x_ref[0].astype(MXU_DTYPE)

    hm = _dot(xb_ref[...], w_ref[0])

    def rope(h):
        xh = hm[:, h * HEAD_DIM:(h + 1) * HEAD_DIM]
        return (xh * cf_ref[0] + pltpu.roll(xh, half, 1) * s1_ref[0]
                + pltpu.roll(xh, LANES - half, 1) * s2_ref[0])

    @pl.when(j == 0)
    def _():
        for h in range(MOBA_HEADS):
            q = rope(h) * (MOBA_SCALE * LOG2E)
            for blk in range(nblk):
                qT_ref[0, h, blk] = q[blk * ATTN_TILE:(blk + 1) * ATTN_TILE].T.astype(MXU_DTYPE)

    @pl.when(j == 1)
    def _():
        for h in range(MOBA_HEADS):
            k = rope(h)
            k_ref[0, h] = k.astype(MXU_DTYPE)
            for blk in range(nblk):
                mean = jnp.mean(k[blk * MOBA_BLOCK:(blk + 1) * MOBA_BLOCK], axis=0, keepdims=True)
                kmean_ref[0, h, pl.ds(i * nblk + blk, 1), :] = mean

    @pl.when(j == 2)
    def _():
        for h in range(MOBA_HEADS):
            v = hm[:, h * HEAD_DIM:(h + 1) * HEAD_DIM]
            for blk in range(nblk):
                _store_vT(vT_ref, h, blk, v[blk * ATTN_TILE:(blk + 1) * ATTN_TILE])


def _moba_prep(x, w_m, cf, s1, s2):
    B, S, D = x.shape
    rows = min(MOBA_PREP_ROWS, S)
    nb = S // ATTN_TILE
    H = MOBA_HEADS
    tspec = pl.BlockSpec((1, rows, LANES), lambda b, i, j: (b, i, 0))
    tile_t = lambda d: pl.BlockSpec((1, H, rows // ATTN_TILE, d, ATTN_TILE), lambda b, i, j: (b, 0, i, 0, 0))
    shape_t = lambda d: jax.ShapeDtypeStruct((B, H, nb, d, ATTN_TILE), MXU_DTYPE)
    return pl.pallas_call(
        _moba_prep_kernel,
        grid=(B, S // rows, 3),
        in_specs=[pl.BlockSpec((1, rows, D), lambda b, i, j: (b, i, 0)),
                  pl.BlockSpec((1, D, H * HEAD_DIM), lambda b, i, j: (j, 0, 0)),
                  tspec, tspec, tspec],
        out_specs=[tile_t(HEAD_DIM),
                   pl.BlockSpec((1, H, rows, HEAD_DIM), lambda b, i, j: (b, 0, i, 0)),
                   tile_t(VT_ROWS),
                   pl.BlockSpec((1, H, nb, HEAD_DIM), lambda b, i, j: (b, 0, 0, 0))],
        out_shape=[shape_t(HEAD_DIM),
                   jax.ShapeDtypeStruct((B, H, S, HEAD_DIM), MXU_DTYPE),
                   shape_t(VT_ROWS),
                   jax.ShapeDtypeStruct((B, H, nb, HEAD_DIM), jnp.float32)],
        scratch_shapes=[pltpu.VMEM((rows, D), MXU_DTYPE)],
        compiler_params=_params("parallel", "arbitrary", "arbitrary"),
        name="moba_prep",
    )(x, w_m, cf, s1, s2)


def _attn_kernel(*refs, gated):
    if gated:
        qT_ref, k_ref, vT_ref, kmean_ref, o_ref, acc_ref, s_ref, p_ref, bias_ref = refs
    else:
        qT_ref, k_ref, vT_ref, o_ref, acc_ref, s_ref, p_ref = refs
    i = pl.program_id(2)
    T = ATTN_TILE
    G = qT_ref.shape[1]
    nb, dv = vT_ref.shape[2], HEAD_DIM
    neg_inf = -jnp.inf

    if gated:
        nb = kmean_ref.shape[2]
        gates = [_dot(kmean_ref[0, g].astype(MXU_DTYPE), qT_ref[0, g, 0]) for g in range(G)]
        for g in range(G):
            gate = gates[g]
            row = lax.broadcasted_iota(jnp.int32, gate.shape, 0)
            past = row < i
            gate = jnp.where(past, gate, neg_inf)
            rank = jnp.zeros(gate.shape, jnp.int32)
            for jp in range(nb):
                gj = gate[jp:jp + 1, :]
                beats = (gj > gate) | ((gj == gate) & (jp < row))
                rank = rank + beats.astype(jnp.int32)
            sel = past & (rank < MOBA_TOPK)
            bias_ref[g] = jnp.where(sel, 0.0, neg_inf)

    heads = range(G)

    def scores(g, j):
        kb = k_ref[0, g, pl.ds(pl.multiple_of(j * T, T), T), :]
        return _dot(kb, qT_ref[0, g, 0])

    def pv(g, j, p):
        return _dot(vT_ref[0, g, j], p)

    s_own = [scores(g, i) for g in heads]
    for g in heads:
        s_ref[0, g] = scores(g, 0)
    kpos = lax.broadcasted_iota(jnp.int32, (T, T), 0)
    qpos = lax.broadcasted_iota(jnp.int32, (T, T), 1)
    causal = kpos <= qpos
    ms = []
    for g in heads:
        s = jnp.where(causal, s_own[g], neg_inf)
        m = jnp.max(s, axis=0, keepdims=True)
        ms.append(m)
        p_ref[g] = jnp.exp2(s - m).astype(MXU_DTYPE)
        acc_ref[g] = jnp.zeros(acc_ref.shape[1:], jnp.float32)
    ones = tuple(jnp.ones((1, T), jnp.float32) for _ in heads)

    def stage(t, slot, carry):
        a_pend, ms, j_pend = carry
        pvs = [pv(g, j_pend, p_ref[g]) for g in heads]
        nxt = jnp.minimum(t + 1, nb - 1)
        for g in heads:
            s_ref[1 - slot, g] = scores(g, nxt)
        new_ms, new_as = [], []
        for g in heads:
            cm = jnp.max(s_ref[slot, g], axis=0, keepdims=True)
            if gated:
                b = bias_ref[g, pl.ds(t, 1), :]
            else:
                b = jnp.where(t < i, 0.0, neg_inf)
            m_new = jnp.maximum(ms[g], cm + b)
            alpha = jnp.exp2(ms[g] - m_new)
            p = jnp.exp2(s_ref[slot, g] - (m_new - b))
            new_ms.append(m_new)
            new_as.append(alpha)
            acc_ref[g] = a_pend[g] * acc_ref[g] + pvs[g]
            p_ref[g] = p.astype(MXU_DTYPE)
        return tuple(new_as), tuple(new_ms), t

    def body(u, carry):
        return stage(2 * u + 1, 1, stage(2 * u, 0, carry))

    carry = (ones, tuple(ms), i)
    a_pend, _, j_pend = lax.fori_loop(0, (i + 1) // 2, body, carry)
    for g in heads:
        acc = a_pend[g] * acc_ref[g] + pv(g, j_pend, p_ref[g])
        o_ref[0, :, g * dv:(g + 1) * dv] = (acc[0:dv] / acc[dv:dv + 1]).T.astype(o_ref.dtype)


def _attention(qT, k, vT, kmean):
    B, H, nb, d, T = qT.shape
    S = nb * T
    dv = HEAD_DIM
    G = ATTN_HEADS
    gated = kmean is not None
    in_specs = [pl.BlockSpec((1, G, 1, d, T), lambda b, h, i: (b, h, i, 0, 0)),
                pl.BlockSpec((1, G, S, d), lambda b, h, i: (b, h, 0, 0)),
                pl.BlockSpec((1, G, nb, VT_ROWS, T), lambda b, h, i: (b, h, 0, 0, 0))]
    args = [qT, k, vT]
    scratch = [pltpu.VMEM((G, VT_ROWS, T), jnp.float32),
               pltpu.VMEM((2, G, T, T), jnp.float32),
               pltpu.VMEM((G, T, T), MXU_DTYPE)]
    if gated:
        in_specs.append(pl.BlockSpec((1, G, nb, d), lambda b, h, i: (b, h, 0, 0)))
        args.append(kmean)
        scratch.append(pltpu.VMEM((G, nb, T), jnp.float32))
    return pl.pallas_call(
        functools.partial(_attn_kernel, gated=gated),
        grid=(B, H // G, nb),
        in_specs=in_specs,
        out_specs=pl.BlockSpec((1, T, G * dv), lambda b, h, i: (b, i, h)),
        out_shape=jax.ShapeDtypeStruct((B, S, H * dv), MXU_DTYPE),
        scratch_shapes=scratch,
        compiler_params=_params("parallel", "parallel", "parallel"),
        name="moba_attention" if gated else "mla_attention",
    )(*args)


def _layer_norm(y, g, b):
    mu = jnp.mean(y, axis=-1, keepdims=True)
    yc = y - mu
    var = jnp.mean(jnp.square(yc), axis=-1, keepdims=True)
    return yc * lax.rsqrt(var + LN_EPS) * g + b


def _outproj_kernel(a1_ref, a2_ref, w1_ref, w2_ref, x_ref, g_ref, b_ref, o_ref):
    mix = _dot(a1_ref[...], w1_ref[...]) + _dot(a2_ref[...], w2_ref[...])
    o_ref[...] = _layer_norm(DEEPNORM_ALPHA * x_ref[...] + mix, g_ref[...], b_ref[...])


def _outproj_ln(a1, a2, w1, w2, x, g, b):
    M, D = x.shape
    rows = min(OUT_ROWS, M)
    K = a1.shape[1]
    full = lambda a: pl.BlockSpec(a.shape, lambda i: (0,) * a.ndim)
    return pl.pallas_call(
        _outproj_kernel,
        grid=(M // rows,),
        in_specs=[pl.BlockSpec((rows, K), lambda i: (i, 0)), pl.BlockSpec((rows, K), lambda i: (i, 0)),
                  full(w1), full(w2), pl.BlockSpec((rows, D), lambda i: (i, 0)), full(g), full(b)],
        out_specs=pl.BlockSpec((rows, D), lambda i: (i, 0)),
        out_shape=jax.ShapeDtypeStruct((M, D), jnp.float32),
        compiler_params=_params("parallel"),
        name="outproj_ln1",
    )(a1, a2, w1, w2, x, g, b)


def _ffn_kernel(x_ref, wup_ref, wdn_ref, g_ref, b_ref, o_ref, xb_ref, acc_ref):
    f = pl.program_id(1)

    @pl.when(f == 0)
    def _():
        xb_ref[...] = x_ref[...].astype(MXU_DTYPE)
        acc_ref[...] = jnp.zeros_like(acc_ref)

    u = jnp.maximum(_dot(xb_ref[...], wup_ref[...]), 0.0)
    acc_ref[...] += _dot(jnp.square(u).astype(MXU_DTYPE), wdn_ref[...])

    @pl.when(f == pl.num_programs(1) - 1)
    def _():
        o_ref[...] = _layer_norm(DEEPNORM_ALPHA * x_ref[...] + acc_ref[...], g_ref[...], b_ref[...])


def _ffn_ln(x, wup, wdn, g, b):
    M, D = x.shape
    F = wup.shape[1]
    rows = min(FFN_ROWS, M)
    cols = min(FFN_COLS, F)
    vec = pl.BlockSpec((1, D), lambda i, f: (0, 0))
    return pl.pallas_call(
        _ffn_kernel,
        grid=(M // rows, F // cols),
        in_specs=[pl.BlockSpec((rows, D), lambda i, f: (i, 0)),
                  pl.BlockSpec((D, cols), lambda i, f: (0, f)),
                  pl.BlockSpec((cols, D), lambda i, f: (f, 0)),
                  vec, vec],
        out_specs=pl.BlockSpec((rows, D), lambda i, f: (i, 0)),
        out_shape=jax.ShapeDtypeStruct((M, D), jnp.float32),
        scratch_shapes=[pltpu.VMEM((rows, D), MXU_DTYPE), pltpu.VMEM((rows, D), jnp.float32)],
        compiler_params=_params("parallel", "arbitrary"),
        name="ffn_ln2",
    )(x, wup, wdn, g, b)


def _layer_weights(w_in, w_uq, w_ukv, w_out, w_up, w_down):
    D = w_in.shape[0]
    half = MLA_ROPE_DIM // 2
    swap = np.concatenate([np.arange(half, MLA_ROPE_DIM), np.arange(half)])
    c0 = MLA_Q_RANK + MLA_KV_RANK
    c1 = c0 + MLA_ROPE_DIM
    mw = MOBA_HEADS * HEAD_DIM
    w_lat = jnp.concatenate([w_in[:, :c1], w_in[:, c0:c1][:, swap]], axis=1)
    w_m = jnp.stack([w_in[:, c1:c1 + mw], w_in[:, c1 + mw:c1 + 2 * mw], w_in[:, c1 + 2 * mw:c1 + 3 * mw]])
    uq = w_uq.reshape(MLA_Q_RANK, MLA_HEADS, HEAD_DIM + MLA_ROPE_DIM)
    rope_cols = uq[:, :, HEAD_DIM:]
    uq = jnp.concatenate([uq, rope_cols[:, :, swap]], axis=-1).reshape(MLA_Q_RANK, MLA_HEADS * 2 * HEAD_DIM)
    ukv = w_ukv.reshape(MLA_KV_RANK, MLA_HEADS, 2, HEAD_DIM).transpose(0, 2, 1, 3).reshape(MLA_KV_RANK, -1)
    n1 = MLA_HEADS * HEAD_DIM
    cast = lambda w: w.astype(MXU_DTYPE)
    return dict(w_lat=cast(w_lat), w_m=cast(w_m), uq=cast(uq), ukv=cast(ukv),
                wo1=cast(w_out[:n1]), wo2=cast(w_out[n1:]), wup=cast(w_up), wdn=cast(w_down))


def kernel(x, positions, w_in, mla_q_norm, mla_kv_norm, w_uq, w_ukv, w_out, ln1_g, ln1_b, w_up, w_down, ln2_g, ln2_b):
    B, S, D = x.shape
    assert S % ATTN_TILE == 0 and w_in.shape[0] == DEPTH
    tmla, cf, s1, s2 = _rope_tables(positions)
    for l in range(DEPTH):
        w = _layer_weights(w_in[l], w_uq[l], w_ukv[l], w_out[l], w_up[l], w_down[l])
        row = lambda v: v[l].reshape(1, -1)
        qT_a, k_a, vT_a = _mla_prep(x, w["w_lat"], row(mla_q_norm), row(mla_kv_norm), w["uq"], w["ukv"], tmla)
        qT_b, k_b, vT_b, kmean = _moba_prep(x, w["w_m"], cf, s1, s2)
        out_a = _attention(qT_a, k_a, vT_a, None)
        out_b = _attention(qT_b, k_b, vT_b, kmean)
        x1 = _outproj_ln(out_a.reshape(B * S, -1), out_b.reshape(B * S, -1), w["wo1"], w["wo2"],
                         x.reshape(B * S, D), row(ln1_g), row(ln1_b))
        x = _ffn_ln(x1, w["wup"], w["wdn"], row(ln2_g), row(ln2_b)).reshape(B, S, D)
    return x
```

```python
import functools
import math

import numpy as np
import jax
import jax.numpy as jnp
from jax import lax
from jax.experimental import pallas as pl
from jax.experimental.pallas import tpu as pltpu

HEAD_DIM = 128
MLA_HEADS = 8
MOBA_HEADS = 8
MLA_Q_RANK = 384
MLA_KV_RANK = 256
MLA_ROPE_DIM = 64
MOBA_BLOCK = 256
MOBA_TOPK = 3
ROPE_THETA = 500000.0
PARTIAL_ROT_DIM = 32
LN_EPS = 1e-5
RMS_EPS = 1e-6
DEPTH = 1
DEEPNORM_ALPHA = (2 * DEPTH) ** 0.25
MLA_SCALE = 1.0 / math.sqrt(HEAD_DIM + MLA_ROPE_DIM)
MOBA_SCALE = 1.0 / math.sqrt(HEAD_DIM)
LOG2E = math.log2(math.e)

LANES = 128
MXU_DTYPE = jnp.bfloat16
VMEM_LIMIT_BYTES = 56 * 1024 * 1024

ATTN_TILE = MOBA_BLOCK
ATTN_HEADS = 4
BF16_SUBLANES = 16
VT_ROWS = HEAD_DIM + BF16_SUBLANES
MLA_PREP_ROWS = 512
MOBA_PREP_ROWS = 512
OUT_ROWS = 512
OUT_CHUNK = 256
FFN_ROWS = 512
FFN_COLS = 1024


def _dot(a, b):
    return jnp.dot(a, b, preferred_element_type=jnp.float32)


def _params(*sem):
    return pltpu.CompilerParams(dimension_semantics=sem, vmem_limit_bytes=VMEM_LIMIT_BYTES)


def _inv_freq(dim):
    return ROPE_THETA ** (-jnp.arange(dim // 2, dtype=jnp.float32) * (2.0 / dim))


def _rms(x, g):
    y = x * lax.rsqrt(jnp.mean(jnp.square(x), axis=-1, keepdims=True) + RMS_EPS)
    return y * g


def _store_vT(vT_ref, h, blk, v):
    vT_ref[0, h, blk, 0:HEAD_DIM, :] = v.T.astype(MXU_DTYPE)
    row = lax.broadcasted_iota(jnp.int32, (BF16_SUBLANES, ATTN_TILE), 0)
    vT_ref[0, h, blk, HEAD_DIM:VT_ROWS, :] = jnp.where(row == 0, 1.0, 0.0).astype(MXU_DTYPE)


def _mla_prep_kernel(x_ref, pos_ref, invf_ref, wlat_ref, gq_ref, gkv_ref, wuq_ref, wukv_ref,
                     qT_ref, k_ref, vT_ref):
    rows = x_ref.shape[1]
    T = ATTN_TILE
    nblk = rows // T
    dq = 2 * HEAD_DIM
    xb = x_ref[0].astype(MXU_DTYPE)
    lats = [_dot(xb[blk * T:(blk + 1) * T], wlat_ref[...]) for blk in range(nblk)]
    for blk in range(nblk):
        r = slice(blk * T, (blk + 1) * T)
        lat = lats[blk]
        cq = lat[:, :MLA_Q_RANK]
        ckv = lat[:, MLA_Q_RANK:MLA_Q_RANK + MLA_KV_RANK]
        kr2 = lat[:, MLA_Q_RANK + MLA_KV_RANK:]
        qall = _dot(_rms(cq, gq_ref[...]).astype(MXU_DTYPE), wuq_ref[...])
        kvall = _dot(_rms(ckv, gkv_ref[...]).astype(MXU_DTYPE), wukv_ref[...])
        ang = pos_ref[0, r, :].astype(jnp.float32) * invf_ref[...]
        lane = lax.broadcasted_iota(jnp.int32, ang.shape, 1)
        cos, sin = jnp.cos(ang), jnp.sin(ang)
        tab = jnp.where(lane < MLA_ROPE_DIM, cos, jnp.where(lane < MLA_ROPE_DIM + MLA_ROPE_DIM // 2, -sin, sin))
        t = kr2 * tab
        k_rope = jnp.where(lane < MLA_ROPE_DIM, t + pltpu.roll(t, MLA_ROPE_DIM, 1), 0.0).astype(MXU_DTYPE)
        qall = qall * (MLA_SCALE * LOG2E)
        for h in range(MLA_HEADS):
            nope = qall[:, h * dq:h * dq + HEAD_DIM]
            t = qall[:, h * dq + HEAD_DIM:(h + 1) * dq] * tab
            rope = t + pltpu.roll(t, MLA_ROPE_DIM, 1)
            qT_ref[0, h, blk, 0:HEAD_DIM, :] = nope.T.astype(MXU_DTYPE)
            qT_ref[0, h, blk, HEAD_DIM:dq, :] = rope.T.astype(MXU_DTYPE)
            k_ref[0, h, r, 0:HEAD_DIM] = kvall[:, h * HEAD_DIM:(h + 1) * HEAD_DIM].astype(MXU_DTYPE)
            k_ref[0, h, r, HEAD_DIM:dq] = k_rope
            _store_vT(vT_ref, h, blk, kvall[:, (MLA_HEADS + h) * HEAD_DIM:(MLA_HEADS + h + 1) * HEAD_DIM])


def _mla_prep(x, positions, w_lat, gq, gkv, wuq, wukv):
    B, S, D = x.shape
    rows = min(MLA_PREP_ROWS, S)
    nb = S // ATTN_TILE
    H, dq = MLA_HEADS, 2 * HEAD_DIM
    invf = jnp.tile(_inv_freq(MLA_ROPE_DIM), 2 * LANES // MLA_ROPE_DIM).reshape(1, LANES)
    full = lambda a: pl.BlockSpec(a.shape, lambda b, i: (0,) * a.ndim)
    return pl.pallas_call(
        _mla_prep_kernel,
        grid=(B, S // rows),
        in_specs=[pl.BlockSpec((1, rows, D), lambda b, i: (b, i, 0)),
                  pl.BlockSpec((1, rows, 1), lambda b, i: (b, i, 0)),
                  full(invf), full(w_lat), full(gq), full(gkv), full(wuq), full(wukv)],
        out_specs=[pl.BlockSpec((1, H, rows // ATTN_TILE, dq, ATTN_TILE), lambda b, i: (b, 0, i, 0, 0)),
                   pl.BlockSpec((1, H, rows, dq), lambda b, i: (b, 0, i, 0)),
                   pl.BlockSpec((1, H, rows // ATTN_TILE, VT_ROWS, ATTN_TILE), lambda b, i: (b, 0, i, 0, 0))],
        out_shape=[jax.ShapeDtypeStruct((B, H, nb, dq, ATTN_TILE), MXU_DTYPE),
                   jax.ShapeDtypeStruct((B, H, S, dq), MXU_DTYPE),
                   jax.ShapeDtypeStruct((B, H, nb, VT_ROWS, ATTN_TILE), MXU_DTYPE)],
        compiler_params=_params("parallel", "parallel"),
        name="mla_prep",
    )(x, positions.reshape(B, S, 1), invf, w_lat, gq, gkv, wuq, wukv)


ROT_HALF = PARTIAL_ROT_DIM // 2
ROT_X2 = LANES // 2
MOBA_HEAD_PERM = np.concatenate([np.arange(0, ROT_HALF), np.arange(PARTIAL_ROT_DIM, PARTIAL_ROT_DIM + ROT_X2 - ROT_HALF),
                                 np.arange(ROT_HALF, PARTIAL_ROT_DIM),
                                 np.arange(PARTIAL_ROT_DIM + ROT_X2 - ROT_HALF, HEAD_DIM)])


def _moba_prep_kernel(x_ref, w_ref, posc_ref, posr_ref, invr_ref, invc_ref, qT_ref, k_ref, vT_ref, kmean_ref):
    i = pl.program_id(1)
    rows = x_ref.shape[1]
    nblk = rows // ATTN_TILE
    T = ATTN_TILE
    xb = x_ref[0].astype(MXU_DTYPE)
    ang = posc_ref[0].astype(jnp.float32) * invr_ref[...]
    lane = lax.broadcasted_iota(jnp.int32, ang.shape, 1)
    k_cos = jnp.cos(ang)
    k_sin = jnp.where(lane < ROT_X2, -jnp.sin(ang), jnp.sin(ang))
    ang_t = invc_ref[...] * posr_ref[0].astype(jnp.float32)
    q_cos, q_sin = jnp.cos(ang_t), jnp.sin(ang_t)
    q_scale = MOBA_SCALE * LOG2E
    width = 2 * HEAD_DIM
    per_kind = MOBA_HEADS * HEAD_DIM // width
    for chunk in range(3 * per_kind):
        hm = _dot(xb, w_ref[:, chunk * width:(chunk + 1) * width])
        kind, pair = divmod(chunk, per_kind)
        for hh in range(2):
            h = 2 * pair + hh
            xh = hm[:, hh * HEAD_DIM:(hh + 1) * HEAD_DIM]
            if kind == 0:
                for blk in range(nblk):
                    t = xh[blk * T:(blk + 1) * T].T * q_scale
                    c, s = q_cos[:, blk * T:(blk + 1) * T], q_sin[:, blk * T:(blk + 1) * T]
                    x1, x2 = t[0:ROT_HALF], t[ROT_X2:ROT_X2 + ROT_HALF]
                    qT_ref[0, h, blk, 0:ROT_HALF, :] = (x1 * c - x2 * s).astype(MXU_DTYPE)
                    qT_ref[0, h, blk, ROT_HALF:ROT_X2, :] = t[ROT_HALF:ROT_X2].astype(MXU_DTYPE)
                    qT_ref[0, h, blk, ROT_X2:ROT_X2 + ROT_HALF, :] = (x2 * c + x1 * s).astype(MXU_DTYPE)
                    qT_ref[0, h, blk, ROT_X2 + ROT_HALF:HEAD_DIM, :] = t[ROT_X2 + ROT_HALF:].astype(MXU_DTYPE)
            elif kind == 1:
                k = xh * k_cos + pltpu.roll(xh, ROT_X2, 1) * k_sin
                k_ref[0, h] = k.astype(MXU_DTYPE)
                for blk in range(nblk):
                    mean = jnp.mean(k[blk * MOBA_BLOCK:(blk + 1) * MOBA_BLOCK], axis=0, keepdims=True)
                    kmean_ref[0, h, pl.ds(i * nblk + blk, 1), :] = mean
            else:
                for blk in range(nblk):
                    _store_vT(vT_ref, h, blk, xh[blk * T:(blk + 1) * T])


def _moba_prep(x, positions, w_m):
    B, S, D = x.shape
    rows = min(MOBA_PREP_ROWS, S)
    nb = S // ATTN_TILE
    H = MOBA_HEADS
    invf = _inv_freq(PARTIAL_ROT_DIM)
    gap = jnp.zeros((ROT_X2 - ROT_HALF,), jnp.float32)
    invr = jnp.concatenate([invf, gap, invf, gap]).reshape(1, LANES)
    invc = invf.reshape(ROT_HALF, 1)
    full = lambda a: pl.BlockSpec(a.shape, lambda b, i: (0,) * a.ndim)
    tile_t = lambda d: pl.BlockSpec((1, H, rows // ATTN_TILE, d, ATTN_TILE), lambda b, i: (b, 0, i, 0, 0))
    shape_t = lambda d: jax.ShapeDtypeStruct((B, H, nb, d, ATTN_TILE), MXU_DTYPE)
    return pl.pallas_call(
        _moba_prep_kernel,
        grid=(B, S // rows),
        in_specs=[pl.BlockSpec((1, rows, D), lambda b, i: (b, i, 0)),
                  full(w_m),
                  pl.BlockSpec((1, rows, 1), lambda b, i: (b, i, 0)),
                  pl.BlockSpec((1, 1, rows), lambda b, i: (b, 0, i)),
                  full(invr), full(invc)],
        out_specs=[tile_t(HEAD_DIM),
                   pl.BlockSpec((1, H, rows, HEAD_DIM), lambda b, i: (b, 0, i, 0)),
                   tile_t(VT_ROWS),
                   pl.BlockSpec((1, H, nb, HEAD_DIM), lambda b, i: (b, 0, 0, 0))],
        out_shape=[shape_t(HEAD_DIM),
                   jax.ShapeDtypeStruct((B, H, S, HEAD_DIM), MXU_DTYPE),
                   shape_t(VT_ROWS),
                   jax.ShapeDtypeStruct((B, H, nb, HEAD_DIM), jnp.float32)],
        compiler_params=_params("parallel", "arbitrary"),
        name="moba_prep",
    )(x, w_m, positions.reshape(B, S, 1), positions.reshape(B, 1, S), invr, invc)


def _attn_kernel(*refs, gated):
    if gated:
        qT_ref, k_ref, vT_ref, kmean_ref, o_ref, acc_ref, s_ref, p_ref, bias_ref = refs
    else:
        qT_ref, k_ref, vT_ref, o_ref, acc_ref, s_ref, p_ref = refs
    i = pl.program_id(2)
    T = ATTN_TILE
    G = qT_ref.shape[1]
    nb, dv = vT_ref.shape[2], HEAD_DIM
    neg_inf = -jnp.inf

    if gated:
        nb = kmean_ref.shape[2]
        gates = [_dot(kmean_ref[0, g].astype(MXU_DTYPE), qT_ref[0, g, 0]) for g in range(G)]
        for g in range(G):
            gate = gates[g]
            row = lax.broadcasted_iota(jnp.int32, gate.shape, 0)
            past = row < i
            gate = jnp.where(past, gate, neg_inf)
            rank = jnp.zeros(gate.shape, jnp.int32)
            for jp in range(nb):
                gj = gate[jp:jp + 1, :]
                beats = (gj > gate) | ((gj == gate) & (jp < row))
                rank = rank + beats.astype(jnp.int32)
            sel = past & (rank < MOBA_TOPK)
            bias_ref[g] = jnp.where(sel, 0.0, neg_inf)

    heads = range(G)

    def scores(g, j):
        kb = k_ref[0, g, pl.ds(pl.multiple_of(j * T, T), T), :]
        return _dot(kb, qT_ref[0, g, 0])

    def pv(g, j, p):
        return _dot(vT_ref[0, g, j], p)

    s_own = [scores(g, i) for g in heads]
    for g in heads:
        s_ref[0, g] = scores(g, 0)
    kpos = lax.broadcasted_iota(jnp.int32, (T, T), 0)
    qpos = lax.broadcasted_iota(jnp.int32, (T, T), 1)
    causal = kpos <= qpos
    ms = []
    for g in heads:
        s = jnp.where(causal, s_own[g], neg_inf)
        m = jnp.max(s, axis=0, keepdims=True)
        ms.append(m)
        p_ref[g] = jnp.exp2(s - m).astype(MXU_DTYPE)
        acc_ref[g] = jnp.zeros(acc_ref.shape[1:], jnp.float32)
    ones = tuple(jnp.ones((1, T), jnp.float32) for _ in heads)

    def stage(t, slot, carry):
        a_pend, ms, j_pend = carry
        nxt = jnp.minimum(t + 1, nb - 1)
        pvs = [pv(g, j_pend, p_ref[g]) for g in heads]
        for g in heads:
            s_ref[1 - slot, g] = scores(g, nxt)
        new_ms, new_as = [], []
        for g in heads:
            cm = jnp.max(s_ref[slot, g], axis=0, keepdims=True)
            if gated:
                b = bias_ref[g, pl.ds(t, 1), :]
            else:
                b = jnp.where(t < i, 0.0, neg_inf)
            m_new = jnp.maximum(ms[g], cm + b)
            alpha = jnp.exp2(ms[g] - m_new)
            p = jnp.exp2(s_ref[slot, g] - (m_new - b))
            new_ms.append(m_new)
            new_as.append(alpha)
            acc_ref[g] = a_pend[g] * acc_ref[g] + pvs[g]
            p_ref[g] = p.astype(MXU_DTYPE)
        return tuple(new_as), tuple(new_ms), t

    def body(u, carry):
        return stage(2 * u + 1, 1, stage(2 * u, 0, carry))

    carry = (ones, tuple(ms), i)
    a_pend, _, j_pend = lax.fori_loop(0, (i + 1) // 2, body, carry)
    for g in heads:
        acc = a_pend[g] * acc_ref[g] + pv(g, j_pend, p_ref[g])
        o_ref[0, :, g * dv:(g + 1) * dv] = (acc[0:dv] / acc[dv:dv + 1]).T.astype(o_ref.dtype)


def _attention(qT, k, vT, kmean):
    B, H, nb, d, T = qT.shape
    S = nb * T
    dv = HEAD_DIM
    G = ATTN_HEADS
    gated = kmean is not None
    in_specs = [pl.BlockSpec((1, G, 1, d, T), lambda b, h, i: (b, h, i, 0, 0)),
                pl.BlockSpec((1, G, S, d), lambda b, h, i: (b, h, 0, 0)),
                pl.BlockSpec((1, G, nb, VT_ROWS, T), lambda b, h, i: (b, h, 0, 0, 0))]
    args = [qT, k, vT]
    scratch = [pltpu.VMEM((G, VT_ROWS, T), jnp.float32),
               pltpu.VMEM((2, G, T, T), jnp.float32),
               pltpu.VMEM((G, T, T), MXU_DTYPE)]
    if gated:
        in_specs.append(pl.BlockSpec((1, G, nb, d), lambda b, h, i: (b, h, 0, 0)))
        args.append(kmean)
        scratch.append(pltpu.VMEM((G, nb, T), jnp.float32))
    return pl.pallas_call(
        functools.partial(_attn_kernel, gated=gated),
        grid=(B, H // G, nb),
        in_specs=in_specs,
        out_specs=pl.BlockSpec((1, T, G * dv), lambda b, h, i: (b, i, h)),
        out_shape=jax.ShapeDtypeStruct((B, S, H * dv), MXU_DTYPE),
        scratch_shapes=scratch,
        compiler_params=_params("parallel", "parallel", "parallel"),
        name="moba_attention" if gated else "mla_attention",
    )(*args)


def _layer_norm(y, g, b):
    mu = jnp.mean(y, axis=-1, keepdims=True)
    yc = y - mu
    var = jnp.mean(jnp.square(yc), axis=-1, keepdims=True)
    return yc * lax.rsqrt(var + LN_EPS) * g + b


def _outproj_kernel(a1_ref, a2_ref, w1_ref, w2_ref, x_ref, g_ref, b_ref, o_ref):
    step = OUT_CHUNK
    for r0 in range(0, x_ref.shape[0], step):
        r = slice(r0, r0 + step)
        mix = _dot(a1_ref[r, :], w1_ref[...]) + _dot(a2_ref[r, :], w2_ref[...])
        o_ref[r, :] = _layer_norm(DEEPNORM_ALPHA * x_ref[r, :] + mix, g_ref[...], b_ref[...])


def _outproj_ln(a1, a2, w1, w2, x, g, b):
    M, D = x.shape
    rows = min(OUT_ROWS, M)
    K = a1.shape[1]
    full = lambda a: pl.BlockSpec(a.shape, lambda i: (0,) * a.ndim)
    return pl.pallas_call(
        _outproj_kernel,
        grid=(M // rows,),
        in_specs=[pl.BlockSpec((rows, K), lambda i: (i, 0)), pl.BlockSpec((rows, K), lambda i: (i, 0)),
                  full(w1), full(w2), pl.BlockSpec((rows, D), lambda i: (i, 0)), full(g), full(b)],
        out_specs=pl.BlockSpec((rows, D), lambda i: (i, 0)),
        out_shape=jax.ShapeDtypeStruct((M, D), jnp.float32),
        compiler_params=_params("parallel"),
        name="outproj_ln1",
    )(a1, a2, w1, w2, x, g, b)


def _ffn_kernel(x_ref, wup_ref, wdn_ref, g_ref, b_ref, o_ref, xb_ref, acc_ref):
    f = pl.program_id(1)

    @pl.when(f == 0)
    def _():
        xb_ref[...] = x_ref[...].astype(MXU_DTYPE)
        acc_ref[...] = jnp.zeros_like(acc_ref)

    u = jnp.maximum(_dot(xb_ref[...], wup_ref[...]), 0.0)
    acc_ref[...] += _dot(jnp.square(u).astype(MXU_DTYPE), wdn_ref[...])

    @pl.when(f == pl.num_programs(1) - 1)
    def _():
        o_ref[...] = _layer_norm(DEEPNORM_ALPHA * x_ref[...] + acc_ref[...], g_ref[...], b_ref[...])


def _ffn_ln(x, wup, wdn, g, b):
    M, D = x.shape
    F = wup.shape[1]
    rows = min(FFN_ROWS, M)
    cols = min(FFN_COLS, F)
    vec = pl.BlockSpec((1, D), lambda i, f: (0, 0))
    return pl.pallas_call(
        _ffn_kernel,
        grid=(M // rows, F // cols),
        in_specs=[pl.BlockSpec((rows, D), lambda i, f: (i, 0)),
                  pl.BlockSpec((D, cols), lambda i, f: (0, f)),
                  pl.BlockSpec((cols, D), lambda i, f: (f, 0)),
                  vec, vec],
        out_specs=pl.BlockSpec((rows, D), lambda i, f: (i, 0)),
        out_shape=jax.ShapeDtypeStruct((M, D), jnp.float32),
        scratch_shapes=[pltpu.VMEM((rows, D), MXU_DTYPE), pltpu.VMEM((rows, D), jnp.float32)],
        compiler_params=_params("parallel", "arbitrary"),
        name="ffn_ln2",
    )(x, wup, wdn, g, b)


def _layer_weights(w_in, w_uq, w_ukv, w_out, w_up, w_down):
    D = w_in.shape[0]
    half = MLA_ROPE_DIM // 2
    swap = np.concatenate([np.arange(half, MLA_ROPE_DIM), np.arange(half)])
    c0 = MLA_Q_RANK + MLA_KV_RANK
    c1 = c0 + MLA_ROPE_DIM
    mw = MOBA_HEADS * HEAD_DIM
    w_lat = jnp.concatenate([w_in[:, :c1], w_in[:, c0:c1][:, swap]], axis=1)
    head_perm = lambda w: w.reshape(D, MOBA_HEADS, HEAD_DIM)[:, :, MOBA_HEAD_PERM].reshape(D, mw)
    w_m = jnp.concatenate([head_perm(w_in[:, c1:c1 + mw]), head_perm(w_in[:, c1 + mw:c1 + 2 * mw]),
                           w_in[:, c1 + 2 * mw:c1 + 3 * mw]], axis=1)
    uq = w_uq.reshape(MLA_Q_RANK, MLA_HEADS, HEAD_DIM + MLA_ROPE_DIM)
    rope_cols = uq[:, :, HEAD_DIM:]
    uq = jnp.concatenate([uq, rope_cols[:, :, swap]], axis=-1).reshape(MLA_Q_RANK, MLA_HEADS * 2 * HEAD_DIM)
    ukv = w_ukv.reshape(MLA_KV_RANK, MLA_HEADS, 2, HEAD_DIM).transpose(0, 2, 1, 3).reshape(MLA_KV_RANK, -1)
    n1 = MLA_HEADS * HEAD_DIM
    cast = lambda w: w.astype(MXU_DTYPE)
    return dict(w_lat=cast(w_lat), w_m=cast(w_m), uq=cast(uq), ukv=cast(ukv),
                wo1=cast(w_out[:n1]), wo2=cast(w_out[n1:]), wup=cast(w_up), wdn=cast(w_down))


def kernel(x, positions, w_in, mla_q_norm, mla_kv_norm, w_uq, w_ukv, w_out, ln1_g, ln1_b, w_up, w_down, ln2_g, ln2_b):
    B, S, D = x.shape
    assert S % ATTN_TILE == 0 and w_in.shape[0] == DEPTH
    for l in range(DEPTH):
        w = _layer_weights(w_in[l], w_uq[l], w_ukv[l], w_out[l], w_up[l], w_down[l])
        row = lambda v: v[l].reshape(1, -1)
        qT_a, k_a, vT_a = _mla_prep(x, positions, w["w_lat"], row(mla_q_norm), row(mla_kv_norm), w["uq"], w["ukv"])
        qT_b, k_b, vT_b, kmean = _moba_prep(x, positions, w["w_m"])
        out_a = _attention(qT_a, k_a, vT_a, None)
        out_b = _attention(qT_b, k_b, vT_b, kmean)
        x1 = _outproj_ln(out_a.reshape(B * S, -1), out_b.reshape(B * S, -1), w["wo1"], w["wo2"],
                         x.reshape(B * S, D), row(ln1_g), row(ln1_b))
        x = _ffn_ln(x1, w["wup"], w["wdn"], row(ln2_g), row(ln2_b)).reshape(B, S, D)
    return x
```

```python
import functools
import math

import numpy as np
import jax
import jax.numpy as jnp
from jax import lax
from jax.experimental import pallas as pl
from jax.experimental.pallas import tpu as pltpu

HEAD_DIM = 128
MLA_HEADS = 8
MOBA_HEADS = 8
MLA_Q_RANK = 384
MLA_KV_RANK = 256
MLA_ROPE_DIM = 64
MOBA_BLOCK = 256
MOBA_TOPK = 3
ROPE_THETA = 500000.0
PARTIAL_ROT_DIM = 32
LN_EPS = 1e-5
RMS_EPS = 1e-6
DEPTH = 1
DEEPNORM_ALPHA = (2 * DEPTH) ** 0.25
MLA_SCALE = 1.0 / math.sqrt(HEAD_DIM + MLA_ROPE_DIM)
MOBA_SCALE = 1.0 / math.sqrt(HEAD_DIM)
LOG2E = math.log2(math.e)

LANES = 128
MXU_DTYPE = jnp.bfloat16
VMEM_LIMIT_BYTES = 56 * 1024 * 1024

ATTN_TILE = MOBA_BLOCK
ATTN_HEADS = 4
BF16_SUBLANES = 16
VT_ROWS = HEAD_DIM + BF16_SUBLANES
MLA_PREP_ROWS = 512
MOBA_PREP_ROWS = 512
OUT_ROWS = 512
OUT_CHUNK = 256
FFN_ROWS = 512
FFN_COLS = 1024
FFN_CHUNK = 256


def _dot(a, b):
    return jnp.dot(a, b, preferred_element_type=jnp.float32)


def _params(*sem):
    return pltpu.CompilerParams(dimension_semantics=sem, vmem_limit_bytes=VMEM_LIMIT_BYTES)


def _inv_freq(dim):
    return ROPE_THETA ** (-jnp.arange(dim // 2, dtype=jnp.float32) * (2.0 / dim))


def _rms(x, g):
    y = x * lax.rsqrt(jnp.mean(jnp.square(x), axis=-1, keepdims=True) + RMS_EPS)
    return y * g


def _store_vT(vT_ref, h, blk, v):
    vT_ref[0, h, blk, 0:HEAD_DIM, :] = v.T.astype(MXU_DTYPE)
    row = lax.broadcasted_iota(jnp.int32, (BF16_SUBLANES, ATTN_TILE), 0)
    vT_ref[0, h, blk, HEAD_DIM:VT_ROWS, :] = jnp.where(row == 0, 1.0, 0.0).astype(MXU_DTYPE)


def _mla_prep_kernel(x_ref, pos_ref, invf_ref, wlat_ref, gq_ref, gkv_ref, wuq_ref, wukv_ref,
                     qT_ref, k_ref, vT_ref):
    rows = x_ref.shape[1]
    T = ATTN_TILE
    nblk = rows // T
    dq = 2 * HEAD_DIM
    xb = x_ref[0].astype(MXU_DTYPE)
    lats = [_dot(xb[blk * T:(blk + 1) * T], wlat_ref[...]) for blk in range(nblk)]
    for blk in range(nblk):
        r = slice(blk * T, (blk + 1) * T)
        lat = lats[blk]
        cq = lat[:, :MLA_Q_RANK]
        ckv = lat[:, MLA_Q_RANK:MLA_Q_RANK + MLA_KV_RANK]
        kr2 = lat[:, MLA_Q_RANK + MLA_KV_RANK:]
        qall = _dot(_rms(cq, gq_ref[...]).astype(MXU_DTYPE), wuq_ref[...])
        kvall = _dot(_rms(ckv, gkv_ref[...]).astype(MXU_DTYPE), wukv_ref[...])
        ang = pos_ref[0, r, :].astype(jnp.float32) * invf_ref[...]
        lane = lax.broadcasted_iota(jnp.int32, ang.shape, 1)
        cos, sin = jnp.cos(ang), jnp.sin(ang)
        tab = jnp.where(lane < MLA_ROPE_DIM, cos, jnp.where(lane < MLA_ROPE_DIM + MLA_ROPE_DIM // 2, -sin, sin))
        t = kr2 * tab
        k_rope = jnp.where(lane < MLA_ROPE_DIM, t + pltpu.roll(t, MLA_ROPE_DIM, 1), 0.0).astype(MXU_DTYPE)
        qall = qall * (MLA_SCALE * LOG2E)
        for h in range(MLA_HEADS):
            nope = qall[:, h * dq:h * dq + HEAD_DIM]
            t = qall[:, h * dq + HEAD_DIM:(h + 1) * dq] * tab
            rope = t + pltpu.roll(t, MLA_ROPE_DIM, 1)
            qT_ref[0, h, blk, 0:HEAD_DIM, :] = nope.T.astype(MXU_DTYPE)
            qT_ref[0, h, blk, HEAD_DIM:dq, :] = rope.T.astype(MXU_DTYPE)
            k_ref[0, h, r, 0:HEAD_DIM] = kvall[:, h * HEAD_DIM:(h + 1) * HEAD_DIM].astype(MXU_DTYPE)
            k_ref[0, h, r, HEAD_DIM:dq] = k_rope
            _store_vT(vT_ref, h, blk, kvall[:, (MLA_HEADS + h) * HEAD_DIM:(MLA_HEADS + h + 1) * HEAD_DIM])


def _mla_prep(x, positions, w_lat, gq, gkv, wuq, wukv):
    B, S, D = x.shape
    rows = min(MLA_PREP_ROWS, S)
    nb = S // ATTN_TILE
    H, dq = MLA_HEADS, 2 * HEAD_DIM
    invf = jnp.tile(_inv_freq(MLA_ROPE_DIM), 2 * LANES // MLA_ROPE_DIM).reshape(1, LANES)
    full = lambda a: pl.BlockSpec(a.shape, lambda b, i: (0,) * a.ndim)
    return pl.pallas_call(
        _mla_prep_kernel,
        grid=(B, S // rows),
        in_specs=[pl.BlockSpec((1, rows, D), lambda b, i: (b, i, 0)),
                  pl.BlockSpec((1, rows, 1), lambda b, i: (b, i, 0)),
                  full(invf), full(w_lat), full(gq), full(gkv), full(wuq), full(wukv)],
        out_specs=[pl.BlockSpec((1, H, rows // ATTN_TILE, dq, ATTN_TILE), lambda b, i: (b, 0, i, 0, 0)),
                   pl.BlockSpec((1, H, rows, dq), lambda b, i: (b, 0, i, 0)),
                   pl.BlockSpec((1, H, rows // ATTN_TILE, VT_ROWS, ATTN_TILE), lambda b, i: (b, 0, i, 0, 0))],
        out_shape=[jax.ShapeDtypeStruct((B, H, nb, dq, ATTN_TILE), MXU_DTYPE),
                   jax.ShapeDtypeStruct((B, H, S, dq), MXU_DTYPE),
                   jax.ShapeDtypeStruct((B, H, nb, VT_ROWS, ATTN_TILE), MXU_DTYPE)],
        compiler_params=_params("parallel", "parallel"),
        name="mla_prep",
    )(x, positions.reshape(B, S, 1), invf, w_lat, gq, gkv, wuq, wukv)


ROT_HALF = PARTIAL_ROT_DIM // 2
ROT_X2 = LANES // 2
MOBA_HEAD_PERM = np.concatenate([np.arange(0, ROT_HALF), np.arange(PARTIAL_ROT_DIM, PARTIAL_ROT_DIM + ROT_X2 - ROT_HALF),
                                 np.arange(ROT_HALF, PARTIAL_ROT_DIM),
                                 np.arange(PARTIAL_ROT_DIM + ROT_X2 - ROT_HALF, HEAD_DIM)])


def _moba_prep_kernel(x_ref, w_ref, posc_ref, posr_ref, invr_ref, invc_ref, qT_ref, k_ref, vT_ref, kmean_ref):
    i = pl.program_id(1)
    rows = x_ref.shape[1]
    nblk = rows // ATTN_TILE
    T = ATTN_TILE
    xb = x_ref[0].astype(MXU_DTYPE)
    ang = posc_ref[0].astype(jnp.float32) * invr_ref[...]
    lane = lax.broadcasted_iota(jnp.int32, ang.shape, 1)
    k_cos = jnp.cos(ang)
    k_sin = jnp.where(lane < ROT_X2, -jnp.sin(ang), jnp.sin(ang))
    ang_t = invc_ref[...] * posr_ref[0].astype(jnp.float32)
    q_cos, q_sin = jnp.cos(ang_t), jnp.sin(ang_t)
    q_scale = MOBA_SCALE * LOG2E
    width = 2 * HEAD_DIM
    per_kind = MOBA_HEADS * HEAD_DIM // width
    for chunk in range(3 * per_kind):
        hm = _dot(xb, w_ref[:, chunk * width:(chunk + 1) * width])
        kind, pair = divmod(chunk, per_kind)
        for hh in range(2):
            h = 2 * pair + hh
            xh = hm[:, hh * HEAD_DIM:(hh + 1) * HEAD_DIM]
            if kind == 0:
                for blk in range(nblk):
                    t = xh[blk * T:(blk + 1) * T].T * q_scale
                    c, s = q_cos[:, blk * T:(blk + 1) * T], q_sin[:, blk * T:(blk + 1) * T]
                    x1, x2 = t[0:ROT_HALF], t[ROT_HALF:PARTIAL_ROT_DIM]
                    split = PARTIAL_ROT_DIM + ROT_X2 - ROT_HALF
                    qT_ref[0, h, blk, 0:ROT_HALF, :] = (x1 * c - x2 * s).astype(MXU_DTYPE)
                    qT_ref[0, h, blk, ROT_HALF:ROT_X2, :] = t[PARTIAL_ROT_DIM:split].astype(MXU_DTYPE)
                    qT_ref[0, h, blk, ROT_X2:ROT_X2 + ROT_HALF, :] = (x2 * c + x1 * s).astype(MXU_DTYPE)
                    qT_ref[0, h, blk, ROT_X2 + ROT_HALF:HEAD_DIM, :] = t[split:].astype(MXU_DTYPE)
            elif kind == 1:
                k = xh * k_cos + pltpu.roll(xh, ROT_X2, 1) * k_sin
                k_ref[0, h] = k.astype(MXU_DTYPE)
                for blk in range(nblk):
                    mean = jnp.mean(k[blk * MOBA_BLOCK:(blk + 1) * MOBA_BLOCK], axis=0, keepdims=True)
                    kmean_ref[0, h, pl.ds(i * nblk + blk, 1), :] = mean
            else:
                for blk in range(nblk):
                    _store_vT(vT_ref, h, blk, xh[blk * T:(blk + 1) * T])


def _moba_prep(x, positions, w_m):
    B, S, D = x.shape
    rows = min(MOBA_PREP_ROWS, S)
    nb = S // ATTN_TILE
    H = MOBA_HEADS
    invf = _inv_freq(PARTIAL_ROT_DIM)
    gap = jnp.zeros((ROT_X2 - ROT_HALF,), jnp.float32)
    invr = jnp.concatenate([invf, gap, invf, gap]).reshape(1, LANES)
    invc = invf.reshape(ROT_HALF, 1)
    full = lambda a: pl.BlockSpec(a.shape, lambda b, i: (0,) * a.ndim)
    tile_t = lambda d: pl.BlockSpec((1, H, rows // ATTN_TILE, d, ATTN_TILE), lambda b, i: (b, 0, i, 0, 0))
    shape_t = lambda d: jax.ShapeDtypeStruct((B, H, nb, d, ATTN_TILE), MXU_DTYPE)
    return pl.pallas_call(
        _moba_prep_kernel,
        grid=(B, S // rows),
        in_specs=[pl.BlockSpec((1, rows, D), lambda b, i: (b, i, 0)),
                  full(w_m),
                  pl.BlockSpec((1, rows, 1), lambda b, i: (b, i, 0)),
                  pl.BlockSpec((1, 1, rows), lambda b, i: (b, 0, i)),
                  full(invr), full(invc)],
        out_specs=[tile_t(HEAD_DIM),
                   pl.BlockSpec((1, H, rows, HEAD_DIM), lambda b, i: (b, 0, i, 0)),
                   tile_t(VT_ROWS),
                   pl.BlockSpec((1, H, nb, HEAD_DIM), lambda b, i: (b, 0, 0, 0))],
        out_shape=[shape_t(HEAD_DIM),
                   jax.ShapeDtypeStruct((B, H, S, HEAD_DIM), MXU_DTYPE),
                   shape_t(VT_ROWS),
                   jax.ShapeDtypeStruct((B, H, nb, HEAD_DIM), jnp.float32)],
        compiler_params=_params("parallel", "arbitrary"),
        name="moba_prep",
    )(x, w_m, positions.reshape(B, S, 1), positions.reshape(B, 1, S), invr, invc)


def _attn_kernel(*refs, gated):
    if gated:
        qT_ref, k_ref, vT_ref, kmean_ref, o_ref, acc_ref, s_ref, p_ref, bias_ref = refs
    else:
        qT_ref, k_ref, vT_ref, o_ref, acc_ref, s_ref, p_ref = refs
    i = pl.program_id(2)
    T = ATTN_TILE
    G = qT_ref.shape[1]
    nb, dv = vT_ref.shape[2], HEAD_DIM
    neg_inf = -jnp.inf

    if gated:
        nb = kmean_ref.shape[2]
        gates = [_dot(kmean_ref[0, g].astype(MXU_DTYPE), qT_ref[0, g, 0]) for g in range(G)]
        for g in range(G):
            gate = gates[g]
            row = lax.broadcasted_iota(jnp.int32, gate.shape, 0)
            past = row < i
            gate = jnp.where(past, gate, neg_inf)
            taken = row < 0
            for _ in range(min(MOBA_TOPK, nb)):
                cand = jnp.where(taken, neg_inf, gate)
                best = jnp.max(cand, axis=0, keepdims=True)
                hit = (cand == best) & jnp.logical_not(taken)
                first = jnp.min(jnp.where(hit, row, nb), axis=0, keepdims=True)
                taken = taken | (row == first)
            bias_ref[g] = jnp.where(taken & past, 0.0, neg_inf)

    heads = range(G)

    def scores(g, j):
        kb = k_ref[0, g, pl.ds(pl.multiple_of(j * T, T), T), :]
        return _dot(kb, qT_ref[0, g, 0])

    def pv(g, j, p):
        return _dot(vT_ref[0, g, j], p)

    s_own = [scores(g, i) for g in heads]
    for g in heads:
        s_ref[0, g] = scores(g, 0)
    kpos = lax.broadcasted_iota(jnp.int32, (T, T), 0)
    qpos = lax.broadcasted_iota(jnp.int32, (T, T), 1)
    causal = kpos <= qpos
    ms = []
    for g in heads:
        s = jnp.where(causal, s_own[g], neg_inf)
        m = jnp.max(s, axis=0, keepdims=True)
        ms.append(m)
        p_ref[g] = jnp.exp2(s - m).astype(MXU_DTYPE)
        acc_ref[g] = jnp.zeros(acc_ref.shape[1:], jnp.float32)
    ones = tuple(jnp.ones((1, T), jnp.float32) for _ in heads)

    def stage(t, slot, carry):
        a_pend, ms, j_pend = carry
        nxt = jnp.minimum(t + 1, nb - 1)
        pvs = [pv(g, j_pend, p_ref[g]) for g in heads]
        for g in heads:
            s_ref[1 - slot, g] = scores(g, nxt)
        new_ms, new_as = [], []
        for g in heads:
            cm = jnp.max(s_ref[slot, g], axis=0, keepdims=True)
            if gated:
                b = bias_ref[g, pl.ds(t, 1), :]
            else:
                b = jnp.where(t < i, 0.0, neg_inf)
            m_new = jnp.maximum(ms[g], cm + b)
            alpha = jnp.exp2(ms[g] - m_new)
            p = jnp.exp2(s_ref[slot, g] - (m_new - b))
            new_ms.append(m_new)
            new_as.append(alpha)
            acc_ref[g] = a_pend[g] * acc_ref[g] + pvs[g]
            p_ref[g] = p.astype(MXU_DTYPE)
        return tuple(new_as), tuple(new_ms), t

    def body(u, carry):
        return stage(2 * u + 1, 1, stage(2 * u, 0, carry))

    carry = (ones, tuple(ms), i)
    a_pend, _, j_pend = lax.fori_loop(0, (i + 1) // 2, body, carry)
    for g in heads:
        acc = a_pend[g] * acc_ref[g] + pv(g, j_pend, p_ref[g])
        o_ref[0, :, g * dv:(g + 1) * dv] = (acc[0:dv] / acc[dv:dv + 1]).T.astype(o_ref.dtype)


def _attention(qT, k, vT, kmean):
    B, H, nb, d, T = qT.shape
    S = nb * T
    dv = HEAD_DIM
    G = ATTN_HEADS
    gated = kmean is not None
    in_specs = [pl.BlockSpec((1, G, 1, d, T), lambda b, h, i: (b, h, i, 0, 0)),
                pl.BlockSpec((1, G, S, d), lambda b, h, i: (b, h, 0, 0)),
                pl.BlockSpec((1, G, nb, VT_ROWS, T), lambda b, h, i: (b, h, 0, 0, 0))]
    args = [qT, k, vT]
    scratch = [pltpu.VMEM((G, VT_ROWS, T), jnp.float32),
               pltpu.VMEM((2, G, T, T), jnp.float32),
               pltpu.VMEM((G, T, T), MXU_DTYPE)]
    if gated:
        in_specs.append(pl.BlockSpec((1, G, nb, d), lambda b, h, i: (b, h, 0, 0)))
        args.append(kmean)
        scratch.append(pltpu.VMEM((G, nb, T), jnp.float32))
    return pl.pallas_call(
        functools.partial(_attn_kernel, gated=gated),
        grid=(B, H // G, nb),
        in_specs=in_specs,
        out_specs=pl.BlockSpec((1, T, G * dv), lambda b, h, i: (b, i, h)),
        out_shape=jax.ShapeDtypeStruct((B, S, H * dv), MXU_DTYPE),
        scratch_shapes=scratch,
        compiler_params=_params("parallel", "parallel", "parallel"),
        name="moba_attention" if gated else "mla_attention",
    )(*args)


def _layer_norm(y, g, b):
    mu = jnp.mean(y, axis=-1, keepdims=True)
    yc = y - mu
    var = jnp.mean(jnp.square(yc), axis=-1, keepdims=True)
    return yc * lax.rsqrt(var + LN_EPS) * g + b


def _outproj_kernel(a1_ref, a2_ref, w_ref, x_ref, g_ref, b_ref, o_ref):
    step = OUT_CHUNK
    k1 = a1_ref.shape[1]
    for r0 in range(0, x_ref.shape[0], step):
        r = slice(r0, r0 + step)
        mix = _dot(a1_ref[r, :], w_ref[0:k1, :]) + _dot(a2_ref[r, :], w_ref[k1:, :])
        o_ref[r, :] = _layer_norm(DEEPNORM_ALPHA * x_ref[r, :] + mix, g_ref[...], b_ref[...])


def _outproj_ln(a1, a2, w, x, g, b):
    M, D = x.shape
    rows = min(OUT_ROWS, M)
    K = a1.shape[1]
    full = lambda a: pl.BlockSpec(a.shape, lambda i: (0,) * a.ndim)
    return pl.pallas_call(
        _outproj_kernel,
        grid=(M // rows,),
        in_specs=[pl.BlockSpec((rows, K), lambda i: (i, 0)), pl.BlockSpec((rows, K), lambda i: (i, 0)),
                  full(w), pl.BlockSpec((rows, D), lambda i: (i, 0)), full(g), full(b)],
        out_specs=pl.BlockSpec((rows, D), lambda i: (i, 0)),
        out_shape=jax.ShapeDtypeStruct((M, D), jnp.float32),
        compiler_params=_params("parallel"),
        name="outproj_ln1",
    )(a1, a2, w, x, g, b)


def _ffn_kernel(x_ref, wup_ref, wdn_ref, g_ref, b_ref, o_ref, xb_ref, acc_ref):
    f = pl.program_id(1)
    last = pl.num_programs(1) - 1

    def partial(r):
        u = jnp.maximum(_dot(xb_ref[r, :], wup_ref[...]), 0.0)
        return _dot(jnp.square(u).astype(MXU_DTYPE), wdn_ref[...])

    @pl.when(f == 0)
    def _():
        xb_ref[...] = x_ref[...].astype(MXU_DTYPE)
        acc_ref[...] = partial(slice(None))

    @pl.when((f > 0) & (f < last))
    def _():
        acc_ref[...] += partial(slice(None))

    @pl.when(f == last)
    def _():
        for r0 in range(0, x_ref.shape[0], FFN_CHUNK):
            r = slice(r0, r0 + FFN_CHUNK)
            y = acc_ref[r, :] + partial(r)
            o_ref[r, :] = _layer_norm(DEEPNORM_ALPHA * x_ref[r, :] + y, g_ref[...], b_ref[...])


def _ffn_ln(x, wup, wdn, g, b):
    M, D = x.shape
    F = wup.shape[1]
    rows = min(FFN_ROWS, M)
    cols = min(FFN_COLS, F)
    vec = pl.BlockSpec((1, D), lambda i, f: (0, 0))
    return pl.pallas_call(
        _ffn_kernel,
        grid=(M // rows, F // cols),
        in_specs=[pl.BlockSpec((rows, D), lambda i, f: (i, 0)),
                  pl.BlockSpec((D, cols), lambda i, f: (0, f)),
                  pl.BlockSpec((cols, D), lambda i, f: (f, 0)),
                  vec, vec],
        out_specs=pl.BlockSpec((rows, D), lambda i, f: (i, 0)),
        out_shape=jax.ShapeDtypeStruct((M, D), jnp.float32),
        scratch_shapes=[pltpu.VMEM((rows, D), MXU_DTYPE), pltpu.VMEM((rows, D), jnp.float32)],
        compiler_params=_params("parallel", "arbitrary"),
        name="ffn_ln2",
    )(x, wup, wdn, g, b)


def _layer_weights(w_in, w_uq, w_ukv, w_out, w_up, w_down):
    D = w_in.shape[0]
    half = MLA_ROPE_DIM // 2
    swap = np.concatenate([np.arange(half, MLA_ROPE_DIM), np.arange(half)])
    c0 = MLA_Q_RANK + MLA_KV_RANK
    c1 = c0 + MLA_ROPE_DIM
    mw = MOBA_HEADS * HEAD_DIM
    cast = lambda w: w.astype(MXU_DTYPE)
    w_in, w_uq, w_ukv = cast(w_in), cast(w_uq), cast(w_ukv)
    w_lat = jnp.concatenate([w_in[:, :c1], w_in[:, c0:c1][:, swap]], axis=1)
    k_cols = w_in[:, c1 + mw:c1 + 2 * mw].reshape(D, MOBA_HEADS, HEAD_DIM)[:, :, MOBA_HEAD_PERM].reshape(D, mw)
    w_m = jnp.concatenate([w_in[:, c1:c1 + mw], k_cols, w_in[:, c1 + 2 * mw:c1 + 3 * mw]], axis=1)
    uq = w_uq.reshape(MLA_Q_RANK, MLA_HEADS, HEAD_DIM + MLA_ROPE_DIM)
    rope_cols = uq[:, :, HEAD_DIM:]
    uq = jnp.concatenate([uq, rope_cols[:, :, swap]], axis=-1).reshape(MLA_Q_RANK, MLA_HEADS * 2 * HEAD_DIM)
    ukv = w_ukv.reshape(MLA_KV_RANK, MLA_HEADS, 2, HEAD_DIM).transpose(0, 2, 1, 3).reshape(MLA_KV_RANK, -1)
    return dict(w_lat=w_lat, w_m=w_m, uq=uq, ukv=ukv, wo=cast(w_out), wup=cast(w_up), wdn=cast(w_down))


def kernel(x, positions, w_in, mla_q_norm, mla_kv_norm, w_uq, w_ukv, w_out, ln1_g, ln1_b, w_up, w_down, ln2_g, ln2_b):
    B, S, D = x.shape
    assert S % ATTN_TILE == 0 and w_in.shape[0] == DEPTH
    for l in range(DEPTH):
        w = _layer_weights(w_in[l], w_uq[l], w_ukv[l], w_out[l], w_up[l], w_down[l])
        row = lambda v: v[l].reshape(1, -1)
        qT_a, k_a, vT_a = _mla_prep(x, positions, w["w_lat"], row(mla_q_norm), row(mla_kv_norm), w["uq"], w["ukv"])
        qT_b, k_b, vT_b, kmean = _moba_prep(x, positions, w["w_m"])
        out_a = _attention(qT_a, k_a, vT_a, None)
        out_b = _attention(qT_b, k_b, vT_b, kmean)
        x1 = _outproj_ln(out_a.reshape(B * S, -1), out_b.reshape(B * S, -1), w["wo"],
                         x.reshape(B * S, D), row(ln1_g), row(ln1_b))
        x = _ffn_ln(x1, w["wup"], w["wdn"], row(ln2_g), row(ln2_b)).reshape(B, S, D)
    return x
```

```python
import functools
import math

import numpy as np
import jax
import jax.numpy as jnp
from jax import lax
from jax.experimental import pallas as pl
from jax.experimental.pallas import tpu as pltpu

HEAD_DIM = 128
MLA_HEADS = 8
MOBA_HEADS = 8
MLA_Q_RANK = 384
MLA_KV_RANK = 256
MLA_ROPE_DIM = 64
MOBA_BLOCK = 256
MOBA_TOPK = 3
ROPE_THETA = 500000.0
PARTIAL_ROT_DIM = 32
LN_EPS = 1e-5
RMS_EPS = 1e-6
DEPTH = 1
DEEPNORM_ALPHA = (2 * DEPTH) ** 0.25
MLA_SCALE = 1.0 / math.sqrt(HEAD_DIM + MLA_ROPE_DIM)
MOBA_SCALE = 1.0 / math.sqrt(HEAD_DIM)
LOG2E = math.log2(math.e)

LANES = 128
MXU_DTYPE = jnp.bfloat16
VMEM_LIMIT_BYTES = 56 * 1024 * 1024

ATTN_TILE = MOBA_BLOCK
ATTN_HEADS = 4
MLA_TILES = 8
MOBA_TILES = 16
BF16_SUBLANES = 16
VT_ROWS = HEAD_DIM + BF16_SUBLANES
MLA_PREP_ROWS = 512
MOBA_PREP_ROWS = 512
OUT_ROWS = 512
OUT_CHUNK = 256
FFN_ROWS = 512
FFN_COLS = 1024
FFN_CHUNK = 256


def _dot(a, b):
    return jnp.dot(a, b, preferred_element_type=jnp.float32)


def _params(*sem):
    return pltpu.CompilerParams(dimension_semantics=sem, vmem_limit_bytes=VMEM_LIMIT_BYTES)


def _inv_freq(dim):
    return ROPE_THETA ** (-jnp.arange(dim // 2, dtype=jnp.float32) * (2.0 / dim))


def _rms(x, g):
    y = x * lax.rsqrt(jnp.mean(jnp.square(x), axis=-1, keepdims=True) + RMS_EPS)
    return y * g


def _store_vT(vT_ref, h, blk, v):
    vT_ref[0, h, blk, 0:HEAD_DIM, :] = v.T.astype(MXU_DTYPE)
    row = lax.broadcasted_iota(jnp.int32, (BF16_SUBLANES, ATTN_TILE), 0)
    vT_ref[0, h, blk, HEAD_DIM:VT_ROWS, :] = jnp.where(row == 0, 1.0, 0.0).astype(MXU_DTYPE)


def _mla_prep_kernel(x_ref, pos_ref, invf_ref, wlat_ref, gq_ref, gkv_ref, wuq_ref, wukv_ref,
                     qT_ref, k_ref, vT_ref):
    rows = x_ref.shape[1]
    T = ATTN_TILE
    nblk = rows // T
    dq = 2 * HEAD_DIM
    xb = x_ref[0].astype(MXU_DTYPE)
    lats = [_dot(xb[blk * T:(blk + 1) * T], wlat_ref[...]) for blk in range(nblk)]
    for blk in range(nblk):
        r = slice(blk * T, (blk + 1) * T)
        lat = lats[blk]
        cq = lat[:, :MLA_Q_RANK]
        ckv = lat[:, MLA_Q_RANK:MLA_Q_RANK + MLA_KV_RANK]
        kr2 = lat[:, MLA_Q_RANK + MLA_KV_RANK:]
        qall = _dot(_rms(cq, gq_ref[...]).astype(MXU_DTYPE), wuq_ref[...])
        kvall = _dot(_rms(ckv, gkv_ref[...]).astype(MXU_DTYPE), wukv_ref[...])
        ang = pos_ref[0, r, :].astype(jnp.float32) * invf_ref[...]
        lane = lax.broadcasted_iota(jnp.int32, ang.shape, 1)
        cos, sin = jnp.cos(ang), jnp.sin(ang)
        tab = jnp.where(lane < MLA_ROPE_DIM, cos, jnp.where(lane < MLA_ROPE_DIM + MLA_ROPE_DIM // 2, -sin, sin))
        t = kr2 * tab
        k_rope = jnp.where(lane < MLA_ROPE_DIM, t + pltpu.roll(t, MLA_ROPE_DIM, 1), 0.0).astype(MXU_DTYPE)
        qall = qall * (MLA_SCALE * LOG2E)
        for h in range(MLA_HEADS):
            nope = qall[:, h * dq:h * dq + HEAD_DIM]
            t = qall[:, h * dq + HEAD_DIM:(h + 1) * dq] * tab
            rope = t + pltpu.roll(t, MLA_ROPE_DIM, 1)
            qT_ref[0, h, blk, 0:HEAD_DIM, :] = nope.T.astype(MXU_DTYPE)
            qT_ref[0, h, blk, HEAD_DIM:dq, :] = rope.T.astype(MXU_DTYPE)
            k_ref[0, h, r, 0:HEAD_DIM] = kvall[:, h * HEAD_DIM:(h + 1) * HEAD_DIM].astype(MXU_DTYPE)
            k_ref[0, h, r, HEAD_DIM:dq] = k_rope
            _store_vT(vT_ref, h, blk, kvall[:, (MLA_HEADS + h) * HEAD_DIM:(MLA_HEADS + h + 1) * HEAD_DIM])


def _mla_prep(x, positions, w_lat, gq, gkv, wuq, wukv):
    B, S, D = x.shape
    rows = min(MLA_PREP_ROWS, S)
    nb = S // ATTN_TILE
    H, dq = MLA_HEADS, 2 * HEAD_DIM
    invf = jnp.tile(_inv_freq(MLA_ROPE_DIM), 2 * LANES // MLA_ROPE_DIM).reshape(1, LANES)
    full = lambda a: pl.BlockSpec(a.shape, lambda b, i: (0,) * a.ndim)
    return pl.pallas_call(
        _mla_prep_kernel,
        grid=(B, S // rows),
        in_specs=[pl.BlockSpec((1, rows, D), lambda b, i: (b, i, 0)),
                  pl.BlockSpec((1, rows, 1), lambda b, i: (b, i, 0)),
                  full(invf), full(w_lat), full(gq), full(gkv), full(wuq), full(wukv)],
        out_specs=[pl.BlockSpec((1, H, rows // ATTN_TILE, dq, ATTN_TILE), lambda b, i: (b, 0, i, 0, 0)),
                   pl.BlockSpec((1, H, rows, dq), lambda b, i: (b, 0, i, 0)),
                   pl.BlockSpec((1, H, rows // ATTN_TILE, VT_ROWS, ATTN_TILE), lambda b, i: (b, 0, i, 0, 0))],
        out_shape=[jax.ShapeDtypeStruct((B, H, nb, dq, ATTN_TILE), MXU_DTYPE),
                   jax.ShapeDtypeStruct((B, H, S, dq), MXU_DTYPE),
                   jax.ShapeDtypeStruct((B, H, nb, VT_ROWS, ATTN_TILE), MXU_DTYPE)],
        compiler_params=_params("parallel", "parallel"),
        name="mla_prep",
    )(x, positions.reshape(B, S, 1), invf, w_lat, gq, gkv, wuq, wukv)


ROT_HALF = PARTIAL_ROT_DIM // 2
ROT_X2 = LANES // 2
MOBA_HEAD_PERM = np.concatenate([np.arange(0, ROT_HALF), np.arange(PARTIAL_ROT_DIM, PARTIAL_ROT_DIM + ROT_X2 - ROT_HALF),
                                 np.arange(ROT_HALF, PARTIAL_ROT_DIM),
                                 np.arange(PARTIAL_ROT_DIM + ROT_X2 - ROT_HALF, HEAD_DIM)])


def _moba_prep_kernel(x_ref, w_ref, posc_ref, posr_ref, invr_ref, invc_ref, qT_ref, k_ref, vT_ref, kmean_ref):
    i = pl.program_id(1)
    rows = x_ref.shape[1]
    nblk = rows // ATTN_TILE
    T = ATTN_TILE
    xb = x_ref[0].astype(MXU_DTYPE)
    ang = posc_ref[0].astype(jnp.float32) * invr_ref[...]
    lane = lax.broadcasted_iota(jnp.int32, ang.shape, 1)
    k_cos = jnp.cos(ang)
    k_sin = jnp.where(lane < ROT_X2, -jnp.sin(ang), jnp.sin(ang))
    ang_t = invc_ref[...] * posr_ref[0].astype(jnp.float32)
    q_cos, q_sin = jnp.cos(ang_t), jnp.sin(ang_t)
    q_scale = MOBA_SCALE * LOG2E
    width = 2 * HEAD_DIM
    per_kind = MOBA_HEADS * HEAD_DIM // width
    for chunk in range(3 * per_kind):
        hm = _dot(xb, w_ref[:, chunk * width:(chunk + 1) * width])
        kind, pair = divmod(chunk, per_kind)
        for hh in range(2):
            h = 2 * pair + hh
            xh = hm[:, hh * HEAD_DIM:(hh + 1) * HEAD_DIM]
            if kind == 0:
                for blk in range(nblk):
                    t = xh[blk * T:(blk + 1) * T].T * q_scale
                    c, s = q_cos[:, blk * T:(blk + 1) * T], q_sin[:, blk * T:(blk + 1) * T]
                    x1, x2 = t[0:ROT_HALF], t[ROT_HALF:PARTIAL_ROT_DIM]
                    split = PARTIAL_ROT_DIM + ROT_X2 - ROT_HALF
                    qT_ref[0, h, blk, 0:ROT_HALF, :] = (x1 * c - x2 * s).astype(MXU_DTYPE)
                    qT_ref[0, h, blk, ROT_HALF:ROT_X2, :] = t[PARTIAL_ROT_DIM:split].astype(MXU_DTYPE)
                    qT_ref[0, h, blk, ROT_X2:ROT_X2 + ROT_HALF, :] = (x2 * c + x1 * s).astype(MXU_DTYPE)
                    qT_ref[0, h, blk, ROT_X2 + ROT_HALF:HEAD_DIM, :] = t[split:].astype(MXU_DTYPE)
            elif kind == 1:
                k = xh * k_cos + pltpu.roll(xh, ROT_X2, 1) * k_sin
                k_ref[0, h] = k.astype(MXU_DTYPE)
                for blk in range(nblk):
                    mean = jnp.mean(k[blk * MOBA_BLOCK:(blk + 1) * MOBA_BLOCK], axis=0, keepdims=True)
                    kmean_ref[0, h, pl.ds(i * nblk + blk, 1), :] = mean
            else:
                for blk in range(nblk):
                    _store_vT(vT_ref, h, blk, xh[blk * T:(blk + 1) * T])


def _moba_prep(x, positions, w_m):
    B, S, D = x.shape
    rows = min(MOBA_PREP_ROWS, S)
    nb = S // ATTN_TILE
    H = MOBA_HEADS
    invf = _inv_freq(PARTIAL_ROT_DIM)
    gap = jnp.zeros((ROT_X2 - ROT_HALF,), jnp.float32)
    invr = jnp.concatenate([invf, gap, invf, gap]).reshape(1, LANES)
    invc = invf.reshape(ROT_HALF, 1)
    full = lambda a: pl.BlockSpec(a.shape, lambda b, i: (0,) * a.ndim)
    tile_t = lambda d: pl.BlockSpec((1, H, rows // ATTN_TILE, d, ATTN_TILE), lambda b, i: (b, 0, i, 0, 0))
    shape_t = lambda d: jax.ShapeDtypeStruct((B, H, nb, d, ATTN_TILE), MXU_DTYPE)
    return pl.pallas_call(
        _moba_prep_kernel,
        grid=(B, S // rows),
        in_specs=[pl.BlockSpec((1, rows, D), lambda b, i: (b, i, 0)),
                  full(w_m),
                  pl.BlockSpec((1, rows, 1), lambda b, i: (b, i, 0)),
                  pl.BlockSpec((1, 1, rows), lambda b, i: (b, 0, i)),
                  full(invr), full(invc)],
        out_specs=[tile_t(HEAD_DIM),
                   pl.BlockSpec((1, H, rows, HEAD_DIM), lambda b, i: (b, 0, i, 0)),
                   tile_t(VT_ROWS),
                   pl.BlockSpec((1, H, nb, HEAD_DIM), lambda b, i: (b, 0, 0, 0))],
        out_shape=[shape_t(HEAD_DIM),
                   jax.ShapeDtypeStruct((B, H, S, HEAD_DIM), MXU_DTYPE),
                   shape_t(VT_ROWS),
                   jax.ShapeDtypeStruct((B, H, nb, HEAD_DIM), jnp.float32)],
        compiler_params=_params("parallel", "arbitrary"),
        name="moba_prep",
    )(x, w_m, positions.reshape(B, S, 1), positions.reshape(B, 1, S), invr, invc)


def _attn_kernel(*refs, gated):
    if gated:
        qT_ref, k_ref, vT_ref, kmean_ref, o_ref, acc_ref, s_ref, p_ref, bias_ref = refs
    else:
        qT_ref, k_ref, vT_ref, o_ref, acc_ref, s_ref, p_ref = refs
    T = ATTN_TILE
    G, tiles = qT_ref.shape[1], qT_ref.shape[2]
    nb, dv = vT_ref.shape[2], HEAD_DIM
    base = pl.program_id(2) * tiles
    neg_inf = -jnp.inf
    heads = range(G)
    ones = tuple(jnp.ones((1, T), jnp.float32) for _ in heads)

    def scores(g, il, j):
        kb = k_ref[0, g, pl.ds(pl.multiple_of(j * T, T), T), :]
        return _dot(kb, qT_ref[0, g, il])

    def pv(g, j, p):
        return _dot(vT_ref[0, g, j], p)

    def finish(il, a_pend, j_pend):
        accs = [a_pend[g] * acc_ref[g] + pv(g, j_pend, p_ref[g]) for g in heads]
        rows = pl.ds(pl.multiple_of(il * T, T), T)
        for g in heads:
            o_ref[0, rows, g * dv:(g + 1) * dv] = (accs[g][0:dv] / accs[g][dv:dv + 1]).T.astype(o_ref.dtype)

    def start(il, i):
        if gated:
            gates = [_dot(kmean_ref[0, g].astype(MXU_DTYPE), qT_ref[0, g, il]) for g in heads]
        s_own = [scores(g, il, i) for g in heads]
        for g in heads:
            s_ref[0, g] = scores(g, il, 0)
        if gated:
            for g in heads:
                row = lax.broadcasted_iota(jnp.int32, gates[g].shape, 0)
                past = row < i
                gate = jnp.where(past, gates[g], neg_inf)
                taken = row < 0
                for _ in range(min(MOBA_TOPK, nb)):
                    cand = jnp.where(taken, neg_inf, gate)
                    best = jnp.max(cand, axis=0, keepdims=True)
                    hit = (cand == best) & jnp.logical_not(taken)
                    first = jnp.min(jnp.where(hit, row, nb), axis=0, keepdims=True)
                    taken = taken | (row == first)
                bias_ref[g] = jnp.where(taken & past, 0.0, neg_inf)
        kpos = lax.broadcasted_iota(jnp.int32, (T, T), 0)
        qpos = lax.broadcasted_iota(jnp.int32, (T, T), 1)
        causal = kpos <= qpos
        ms = []
        for g in heads:
            s = jnp.where(causal, s_own[g], neg_inf)
            m = jnp.max(s, axis=0, keepdims=True)
            ms.append(m)
            p_ref[g] = jnp.exp2(s - m).astype(MXU_DTYPE)
            acc_ref[g] = jnp.zeros(acc_ref.shape[1:], jnp.float32)
        return tuple(ms)

    def stage(il, i, t, slot, carry):
        a_pend, ms, j_pend = carry
        nxt = jnp.minimum(t + 1, nb - 1)
        pvs = [pv(g, j_pend, p_ref[g]) for g in heads]
        for g in heads:
            s_ref[1 - slot, g] = scores(g, il, nxt)
        new_ms, new_as = [], []
        for g in heads:
            cm = jnp.max(s_ref[slot, g], axis=0, keepdims=True)
            if gated:
                b = bias_ref[g, pl.ds(t, 1), :]
            else:
                b = jnp.where(t < i, 0.0, neg_inf)
            m_new = jnp.maximum(ms[g], cm + b)
            alpha = jnp.exp2(ms[g] - m_new)
            p = jnp.exp2(s_ref[slot, g] - (m_new - b))
            new_ms.append(m_new)
            new_as.append(alpha)
            acc_ref[g] = a_pend[g] * acc_ref[g] + pvs[g]
            p_ref[g] = p.astype(MXU_DTYPE)
        return tuple(new_as), tuple(new_ms), t

    for g in heads:
        acc_ref[g] = jnp.ones(acc_ref.shape[1:], jnp.float32)
        p_ref[g] = jnp.zeros(p_ref.shape[1:], MXU_DTYPE)

    def tile(il, carry):
        a_pend, j_pend = carry
        finish(jnp.maximum(il - 1, 0), a_pend, j_pend)
        cur = jnp.minimum(il, tiles - 1)
        i = base + cur
        ms = start(cur, i)

        def body(u, c):
            return stage(cur, i, 2 * u + 1, 1, stage(cur, i, 2 * u, 0, c))

        trips = jnp.where(il < tiles, (i + 1) // 2, 0)
        a_pend, _, j_pend = lax.fori_loop(0, trips, body, (ones, ms, i))
        return a_pend, j_pend

    lax.fori_loop(0, tiles + 1, tile, (ones, jnp.int32(0)))


def _attention(qT, k, vT, kmean, tiles):
    B, H, nb, d, T = qT.shape
    S = nb * T
    dv = HEAD_DIM
    G = ATTN_HEADS
    gated = kmean is not None
    in_specs = [pl.BlockSpec((1, G, tiles, d, T), lambda b, h, i: (b, h, i, 0, 0)),
                pl.BlockSpec((1, G, S, d), lambda b, h, i: (b, h, 0, 0)),
                pl.BlockSpec((1, G, nb, VT_ROWS, T), lambda b, h, i: (b, h, 0, 0, 0))]
    args = [qT, k, vT]
    scratch = [pltpu.VMEM((G, VT_ROWS, T), jnp.float32),
               pltpu.VMEM((2, G, T, T), jnp.float32),
               pltpu.VMEM((G, T, T), MXU_DTYPE)]
    if gated:
        in_specs.append(pl.BlockSpec((1, G, nb, d), lambda b, h, i: (b, h, 0, 0)))
        args.append(kmean)
        scratch.append(pltpu.VMEM((G, nb, T), jnp.float32))
    return pl.pallas_call(
        functools.partial(_attn_kernel, gated=gated),
        grid=(B, H // G, nb // tiles),
        in_specs=in_specs,
        out_specs=pl.BlockSpec((1, tiles * T, G * dv), lambda b, h, i: (b, i, h)),
        out_shape=jax.ShapeDtypeStruct((B, S, H * dv), MXU_DTYPE),
        scratch_shapes=scratch,
        compiler_params=_params("parallel", "parallel", "parallel"),
        name="moba_attention" if gated else "mla_attention",
    )(*args)


def _layer_norm(y, g, b):
    mu = jnp.mean(y, axis=-1, keepdims=True)
    yc = y - mu
    var = jnp.mean(jnp.square(yc), axis=-1, keepdims=True)
    return yc * lax.rsqrt(var + LN_EPS) * g + b


def _outproj_kernel(a1_ref, a2_ref, w_ref, x_ref, g_ref, b_ref, o_ref):
    step = OUT_CHUNK
    k1 = a1_ref.shape[1]
    for r0 in range(0, x_ref.shape[0], step):
        r = slice(r0, r0 + step)
        mix = _dot(a1_ref[r, :], w_ref[0:k1, :]) + _dot(a2_ref[r, :], w_ref[k1:, :])
        o_ref[r, :] = _layer_norm(DEEPNORM_ALPHA * x_ref[r, :] + mix, g_ref[...], b_ref[...])


def _outproj_ln(a1, a2, w, x, g, b):
    M, D = x.shape
    rows = min(OUT_ROWS, M)
    K = a1.shape[1]
    full = lambda a: pl.BlockSpec(a.shape, lambda i: (0,) * a.ndim)
    return pl.pallas_call(
        _outproj_kernel,
        grid=(M // rows,),
        in_specs=[pl.BlockSpec((rows, K), lambda i: (i, 0)), pl.BlockSpec((rows, K), lambda i: (i, 0)),
                  full(w), pl.BlockSpec((rows, D), lambda i: (i, 0)), full(g), full(b)],
        out_specs=pl.BlockSpec((rows, D), lambda i: (i, 0)),
        out_shape=jax.ShapeDtypeStruct((M, D), jnp.float32),
        compiler_params=_params("parallel"),
        name="outproj_ln1",
    )(a1, a2, w, x, g, b)


def _ffn_kernel(x_ref, wup_ref, wdn_ref, g_ref, b_ref, o_ref, xb_ref, acc_ref):
    f = pl.program_id(1)
    last = pl.num_programs(1) - 1

    def partial(r):
        u = jnp.maximum(_dot(xb_ref[r, :], wup_ref[...]), 0.0)
        return _dot(jnp.square(u).astype(MXU_DTYPE), wdn_ref[...])

    @pl.when(f == 0)
    def _():
        xb_ref[...] = x_ref[...].astype(MXU_DTYPE)
        acc_ref[...] = partial(slice(None))

    @pl.when((f > 0) & (f < last))
    def _():
        acc_ref[...] += partial(slice(None))

    @pl.when(f == last)
    def _():
        for r0 in range(0, x_ref.shape[0], FFN_CHUNK):
            r = slice(r0, r0 + FFN_CHUNK)
            y = acc_ref[r, :] + partial(r)
            o_ref[r, :] = _layer_norm(DEEPNORM_ALPHA * x_ref[r, :] + y, g_ref[...], b_ref[...])


def _ffn_ln(x, wup, wdn, g, b):
    M, D = x.shape
    F = wup.shape[1]
    rows = min(FFN_ROWS, M)
    cols = min(FFN_COLS, F)
    vec = pl.BlockSpec((1, D), lambda i, f: (0, 0))
    return pl.pallas_call(
        _ffn_kernel,
        grid=(M // rows, F // cols),
        in_specs=[pl.BlockSpec((rows, D), lambda i, f: (i, 0)),
                  pl.BlockSpec((D, cols), lambda i, f: (0, f)),
                  pl.BlockSpec((cols, D), lambda i, f: (f, 0)),
                  vec, vec],
        out_specs=pl.BlockSpec((rows, D), lambda i, f: (i, 0)),
        out_shape=jax.ShapeDtypeStruct((M, D), jnp.float32),
        scratch_shapes=[pltpu.VMEM((rows, D), MXU_DTYPE), pltpu.VMEM((rows, D), jnp.float32)],
        compiler_params=_params("parallel", "arbitrary"),
        name="ffn_ln2",
    )(x, wup, wdn, g, b)


def _layer_weights(w_in, w_uq, w_ukv, w_out, w_up, w_down):
    D = w_in.shape[0]
    half = MLA_ROPE_DIM // 2
    swap = np.concatenate([np.arange(half, MLA_ROPE_DIM), np.arange(half)])
    c0 = MLA_Q_RANK + MLA_KV_RANK
    c1 = c0 + MLA_ROPE_DIM
    mw = MOBA_HEADS * HEAD_DIM
    cast = lambda w: w.astype(MXU_DTYPE)
    w_in, w_uq, w_ukv = cast(w_in), cast(w_uq), cast(w_ukv)
    w_lat = jnp.concatenate([w_in[:, :c1], w_in[:, c0:c1][:, swap]], axis=1)
    k_cols = w_in[:, c1 + mw:c1 + 2 * mw].reshape(D, MOBA_HEADS, HEAD_DIM)[:, :, MOBA_HEAD_PERM].reshape(D, mw)
    w_m = jnp.concatenate([w_in[:, c1:c1 + mw], k_cols, w_in[:, c1 + 2 * mw:c1 + 3 * mw]], axis=1)
    uq = w_uq.reshape(MLA_Q_RANK, MLA_HEADS, HEAD_DIM + MLA_ROPE_DIM)
    rope_cols = uq[:, :, HEAD_DIM:]
    uq = jnp.concatenate([uq, rope_cols[:, :, swap]], axis=-1).reshape(MLA_Q_RANK, MLA_HEADS * 2 * HEAD_DIM)
    ukv = w_ukv.reshape(MLA_KV_RANK, MLA_HEADS, 2, HEAD_DIM).transpose(0, 2, 1, 3).reshape(MLA_KV_RANK, -1)
    return dict(w_lat=w_lat, w_m=w_m, uq=uq, ukv=ukv, wo=cast(w_out), wup=cast(w_up), wdn=cast(w_down))


def kernel(x, positions, w_in, mla_q_norm, mla_kv_norm, w_uq, w_ukv, w_out, ln1_g, ln1_b, w_up, w_down, ln2_g, ln2_b):
    B, S, D = x.shape
    assert S % ATTN_TILE == 0 and w_in.shape[0] == DEPTH
    for l in range(DEPTH):
        w = _layer_weights(w_in[l], w_uq[l], w_ukv[l], w_out[l], w_up[l], w_down[l])
        row = lambda v: v[l].reshape(1, -1)
        qT_a, k_a, vT_a = _mla_prep(x, positions, w["w_lat"], row(mla_q_norm), row(mla_kv_norm), w["uq"], w["ukv"])
        qT_b, k_b, vT_b, kmean = _moba_prep(x, positions, w["w_m"])
        nb = S // ATTN_TILE
        out_a = _attention(qT_a, k_a, vT_a, None, min(MLA_TILES, nb))
        out_b = _attention(qT_b, k_b, vT_b, kmean, min(MOBA_TILES, nb))
        x1 = _outproj_ln(out_a.reshape(B * S, -1), out_b.reshape(B * S, -1), w["wo"],
                         x.reshape(B * S, D), row(ln1_g), row(ln1_b))
        x = _ffn_ln(x1, w["wup"], w["wdn"], row(ln2_g), row(ln2_b)).reshape(B, S, D)
    return x
```

```python
import functools
import math

import numpy as np
import jax
import jax.numpy as jnp
from jax import lax
from jax.experimental import pallas as pl
from jax.experimental.pallas import tpu as pltpu

HEAD_DIM = 128
MLA_HEADS = 8
MOBA_HEADS = 8
MLA_Q_RANK = 384
MLA_KV_RANK = 256
MLA_ROPE_DIM = 64
MOBA_BLOCK = 256
MOBA_TOPK = 3
ROPE_THETA = 500000.0
PARTIAL_ROT_DIM = 32
LN_EPS = 1e-5
RMS_EPS = 1e-6
DEPTH = 1
DEEPNORM_ALPHA = (2 * DEPTH) ** 0.25
MLA_SCALE = 1.0 / math.sqrt(HEAD_DIM + MLA_ROPE_DIM)
MOBA_SCALE = 1.0 / math.sqrt(HEAD_DIM)
LOG2E = math.log2(math.e)

LANES = 128
MXU_DTYPE = jnp.bfloat16
VMEM_LIMIT_BYTES = 56 * 1024 * 1024

ATTN_TILE = MOBA_BLOCK
ATTN_HEADS = 4
MLA_TILES = 8
MOBA_TILES = 16
BF16_SUBLANES = 16
VT_ROWS = HEAD_DIM + BF16_SUBLANES
MLA_PREP_ROWS = 512
MOBA_PREP_ROWS = 512
OUT_ROWS = 512
OUT_CHUNK = 256
FFN_ROWS = 512
FFN_COLS = 1024
FFN_CHUNK = 256


def _dot(a, b):
    return jnp.dot(a, b, preferred_element_type=jnp.float32)


def _params(*sem):
    return pltpu.CompilerParams(dimension_semantics=sem, vmem_limit_bytes=VMEM_LIMIT_BYTES)


def _inv_freq(dim):
    return ROPE_THETA ** (-jnp.arange(dim // 2, dtype=jnp.float32) * (2.0 / dim))


def _rms(x, g):
    y = x * lax.rsqrt(jnp.mean(jnp.square(x), axis=-1, keepdims=True) + RMS_EPS)
    return y * g


def _store_vT(vT_ref, h, blk, v):
    vT_ref[0, h, blk, 0:HEAD_DIM, :] = v.T.astype(MXU_DTYPE)
    row = lax.broadcasted_iota(jnp.int32, (BF16_SUBLANES, ATTN_TILE), 0)
    vT_ref[0, h, blk, HEAD_DIM:VT_ROWS, :] = jnp.where(row == 0, 1.0, 0.0).astype(MXU_DTYPE)


def _mla_prep_kernel(x_ref, pos_ref, invf_ref, wlat_ref, gq_ref, gkv_ref, wuq_ref, wukv_ref,
                     qT_ref, k_ref, vT_ref):
    rows = x_ref.shape[1]
    T = ATTN_TILE
    nblk = rows // T
    dq = 2 * HEAD_DIM
    xb = x_ref[0].astype(MXU_DTYPE)
    lats = [_dot(xb[blk * T:(blk + 1) * T], wlat_ref[...]) for blk in range(nblk)]
    for blk in range(nblk):
        r = slice(blk * T, (blk + 1) * T)
        lat = lats[blk]
        cq = lat[:, :MLA_Q_RANK]
        ckv = lat[:, MLA_Q_RANK:MLA_Q_RANK + MLA_KV_RANK]
        kr2 = lat[:, MLA_Q_RANK + MLA_KV_RANK:]
        qall = _dot(_rms(cq, gq_ref[...]).astype(MXU_DTYPE), wuq_ref[...])
        kvall = _dot(_rms(ckv, gkv_ref[...]).astype(MXU_DTYPE), wukv_ref[...])
        ang = pos_ref[0, r, :].astype(jnp.float32) * invf_ref[...]
        lane = lax.broadcasted_iota(jnp.int32, ang.shape, 1)
        cos, sin = jnp.cos(ang), jnp.sin(ang)
        tab = jnp.where(lane < MLA_ROPE_DIM, cos, jnp.where(lane < MLA_ROPE_DIM + MLA_ROPE_DIM // 2, -sin, sin))
        t = kr2 * tab
        k_rope = jnp.where(lane < MLA_ROPE_DIM, t + pltpu.roll(t, MLA_ROPE_DIM, 1), 0.0).astype(MXU_DTYPE)
        qall = qall * (MLA_SCALE * LOG2E)
        for h in range(MLA_HEADS):
            nope = qall[:, h * dq:h * dq + HEAD_DIM]
            t = qall[:, h * dq + HEAD_DIM:(h + 1) * dq] * tab
            rope = t + pltpu.roll(t, MLA_ROPE_DIM, 1)
            qT_ref[0, h, blk, 0:HEAD_DIM, :] = nope.T.astype(MXU_DTYPE)
            qT_ref[0, h, blk, HEAD_DIM:dq, :] = rope.T.astype(MXU_DTYPE)
            k_ref[0, h, r, 0:HEAD_DIM] = kvall[:, h * HEAD_DIM:(h + 1) * HEAD_DIM].astype(MXU_DTYPE)
            k_ref[0, h, r, HEAD_DIM:dq] = k_rope
            _store_vT(vT_ref, h, blk, kvall[:, (MLA_HEADS + h) * HEAD_DIM:(MLA_HEADS + h + 1) * HEAD_DIM])


def _mla_prep(x, positions, w_lat, gq, gkv, wuq, wukv):
    B, S, D = x.shape
    rows = min(MLA_PREP_ROWS, S)
    nb = S // ATTN_TILE
    H, dq = MLA_HEADS, 2 * HEAD_DIM
    invf = jnp.tile(_inv_freq(MLA_ROPE_DIM), 2 * LANES // MLA_ROPE_DIM).reshape(1, LANES)
    full = lambda a: pl.BlockSpec(a.shape, lambda b, i: (0,) * a.ndim)
    return pl.pallas_call(
        _mla_prep_kernel,
        grid=(B, S // rows),
        in_specs=[pl.BlockSpec((1, rows, D), lambda b, i: (b, i, 0)),
                  pl.BlockSpec((1, rows, 1), lambda b, i: (b, i, 0)),
                  full(invf), full(w_lat), full(gq), full(gkv), full(wuq), full(wukv)],
        out_specs=[pl.BlockSpec((1, H, rows // ATTN_TILE, dq, ATTN_TILE), lambda b, i: (b, 0, i, 0, 0)),
                   pl.BlockSpec((1, H, rows, dq), lambda b, i: (b, 0, i, 0)),
                   pl.BlockSpec((1, H, rows // ATTN_TILE, VT_ROWS, ATTN_TILE), lambda b, i: (b, 0, i, 0, 0))],
        out_shape=[jax.ShapeDtypeStruct((B, H, nb, dq, ATTN_TILE), MXU_DTYPE),
                   jax.ShapeDtypeStruct((B, H, S, dq), MXU_DTYPE),
                   jax.ShapeDtypeStruct((B, H, nb, VT_ROWS, ATTN_TILE), MXU_DTYPE)],
        compiler_params=_params("parallel", "parallel"),
        name="mla_prep",
    )(x, positions.reshape(B, S, 1), invf, w_lat, gq, gkv, wuq, wukv)


ROT_HALF = PARTIAL_ROT_DIM // 2
ROT_X2 = LANES // 2
MOBA_HEAD_PERM = np.concatenate([np.arange(0, ROT_HALF), np.arange(PARTIAL_ROT_DIM, PARTIAL_ROT_DIM + ROT_X2 - ROT_HALF),
                                 np.arange(ROT_HALF, PARTIAL_ROT_DIM),
                                 np.arange(PARTIAL_ROT_DIM + ROT_X2 - ROT_HALF, HEAD_DIM)])


def _moba_prep_kernel(x_ref, w_ref, posc_ref, posr_ref, invr_ref, invc_ref, qT_ref, k_ref, vT_ref, kmean_ref):
    i = pl.program_id(1)
    rows = x_ref.shape[1]
    nblk = rows // ATTN_TILE
    T = ATTN_TILE
    xb = x_ref[0].astype(MXU_DTYPE)
    ang = posc_ref[0].astype(jnp.float32) * invr_ref[...]
    lane = lax.broadcasted_iota(jnp.int32, ang.shape, 1)
    k_cos = jnp.cos(ang)
    k_sin = jnp.where(lane < ROT_X2, -jnp.sin(ang), jnp.sin(ang))
    ang_t = invc_ref[...] * posr_ref[0].astype(jnp.float32)
    q_cos, q_sin = jnp.cos(ang_t), jnp.sin(ang_t)
    q_scale = MOBA_SCALE * LOG2E
    width = 2 * HEAD_DIM
    per_kind = MOBA_HEADS * HEAD_DIM // width
    for chunk in range(3 * per_kind):
        hm = _dot(xb, w_ref[:, chunk * width:(chunk + 1) * width])
        kind, pair = divmod(chunk, per_kind)
        for hh in range(2):
            h = 2 * pair + hh
            xh = hm[:, hh * HEAD_DIM:(hh + 1) * HEAD_DIM]
            if kind == 0:
                for blk in range(nblk):
                    t = xh[blk * T:(blk + 1) * T].T * q_scale
                    c, s = q_cos[:, blk * T:(blk + 1) * T], q_sin[:, blk * T:(blk + 1) * T]
                    x1, x2 = t[0:ROT_HALF], t[ROT_HALF:PARTIAL_ROT_DIM]
                    split = PARTIAL_ROT_DIM + ROT_X2 - ROT_HALF
                    qT_ref[0, h, blk, 0:ROT_HALF, :] = (x1 * c - x2 * s).astype(MXU_DTYPE)
                    qT_ref[0, h, blk, ROT_HALF:ROT_X2, :] = t[PARTIAL_ROT_DIM:split].astype(MXU_DTYPE)
                    qT_ref[0, h, blk, ROT_X2:ROT_X2 + ROT_HALF, :] = (x2 * c + x1 * s).astype(MXU_DTYPE)
                    qT_ref[0, h, blk, ROT_X2 + ROT_HALF:HEAD_DIM, :] = t[split:].astype(MXU_DTYPE)
            elif kind == 1:
                k = xh * k_cos + pltpu.roll(xh, ROT_X2, 1) * k_sin
                k_ref[0, h] = k.astype(MXU_DTYPE)
                for blk in range(nblk):
                    mean = jnp.mean(k[blk * MOBA_BLOCK:(blk + 1) * MOBA_BLOCK], axis=0, keepdims=True)
                    kmean_ref[0, h, pl.ds(i * nblk + blk, 1), :] = mean
            else:
                for blk in range(nblk):
                    _store_vT(vT_ref, h, blk, xh[blk * T:(blk + 1) * T])


def _moba_prep(x, positions, w_m):
    B, S, D = x.shape
    rows = min(MOBA_PREP_ROWS, S)
    nb = S // ATTN_TILE
    H = MOBA_HEADS
    invf = _inv_freq(PARTIAL_ROT_DIM)
    gap = jnp.zeros((ROT_X2 - ROT_HALF,), jnp.float32)
    invr = jnp.concatenate([invf, gap, invf, gap]).reshape(1, LANES)
    invc = invf.reshape(ROT_HALF, 1)
    full = lambda a: pl.BlockSpec(a.shape, lambda b, i: (0,) * a.ndim)
    tile_t = lambda d: pl.BlockSpec((1, H, rows // ATTN_TILE, d, ATTN_TILE), lambda b, i: (b, 0, i, 0, 0))
    shape_t = lambda d: jax.ShapeDtypeStruct((B, H, nb, d, ATTN_TILE), MXU_DTYPE)
    return pl.pallas_call(
        _moba_prep_kernel,
        grid=(B, S // rows),
        in_specs=[pl.BlockSpec((1, rows, D), lambda b, i: (b, i, 0)),
                  full(w_m),
                  pl.BlockSpec((1, rows, 1), lambda b, i: (b, i, 0)),
                  pl.BlockSpec((1, 1, rows), lambda b, i: (b, 0, i)),
                  full(invr), full(invc)],
        out_specs=[tile_t(HEAD_DIM),
                   pl.BlockSpec((1, H, rows, HEAD_DIM), lambda b, i: (b, 0, i, 0)),
                   tile_t(VT_ROWS),
                   pl.BlockSpec((1, H, nb, HEAD_DIM), lambda b, i: (b, 0, 0, 0))],
        out_shape=[shape_t(HEAD_DIM),
                   jax.ShapeDtypeStruct((B, H, S, HEAD_DIM), MXU_DTYPE),
                   shape_t(VT_ROWS),
                   jax.ShapeDtypeStruct((B, H, nb, HEAD_DIM), jnp.float32)],
        compiler_params=_params("parallel", "arbitrary"),
        name="moba_prep",
    )(x, w_m, positions.reshape(B, S, 1), positions.reshape(B, 1, S), invr, invc)


def _attn_kernel(*refs, gated):
    if gated:
        qT_ref, k_ref, vT_ref, kmean_ref, o_ref, acc_ref, s_ref, p_ref, bias_ref = refs
    else:
        qT_ref, k_ref, vT_ref, o_ref, acc_ref, s_ref, p_ref = refs
    T = ATTN_TILE
    G, tiles = qT_ref.shape[1], qT_ref.shape[2]
    nb, dv = vT_ref.shape[2], HEAD_DIM
    base = pl.program_id(2) * tiles
    neg_inf = -jnp.inf
    heads = range(G)
    ones = tuple(jnp.ones((1, T), jnp.float32) for _ in heads)

    def scores(g, il, j):
        kb = k_ref[0, g, pl.ds(pl.multiple_of(j * T, T), T), :]
        return _dot(kb, qT_ref[0, g, il])

    def pv(g, j, p):
        return _dot(vT_ref[0, g, j], p)

    def finish(il, a_pend, j_pend):
        accs = [a_pend[g] * acc_ref[g] + pv(g, j_pend, p_ref[g]) for g in heads]
        rows = pl.ds(pl.multiple_of(il * T, T), T)
        for g in heads:
            o_ref[0, rows, g * dv:(g + 1) * dv] = (accs[g][0:dv] / accs[g][dv:dv + 1]).T.astype(o_ref.dtype)

    def start(il, i):
        if gated:
            gates = [_dot(kmean_ref[0, g].astype(MXU_DTYPE), qT_ref[0, g, il]) for g in heads]
        s_own = [scores(g, il, i) for g in heads]
        cms = []
        for g in heads:
            s_first = scores(g, il, 0)
            s_ref[0, g] = s_first
            cms.append(jnp.max(s_first, axis=0, keepdims=True))
        if gated:
            for g in heads:
                row = lax.broadcasted_iota(jnp.int32, gates[g].shape, 0)
                past = row < i
                gate = jnp.where(past, gates[g], neg_inf)
                taken = row < 0
                for _ in range(min(MOBA_TOPK, nb)):
                    cand = jnp.where(taken, neg_inf, gate)
                    best = jnp.max(cand, axis=0, keepdims=True)
                    hit = (cand == best) & jnp.logical_not(taken)
                    first = jnp.min(jnp.where(hit, row, nb), axis=0, keepdims=True)
                    taken = taken | (row == first)
                bias_ref[g] = jnp.where(taken & past, 0.0, neg_inf)
        kpos = lax.broadcasted_iota(jnp.int32, (T, T), 0)
        qpos = lax.broadcasted_iota(jnp.int32, (T, T), 1)
        causal = kpos <= qpos
        ms = []
        for g in heads:
            s = jnp.where(causal, s_own[g], neg_inf)
            m = jnp.max(s, axis=0, keepdims=True)
            ms.append(m)
            p_ref[g] = jnp.exp2(s - m).astype(MXU_DTYPE)
            acc_ref[g] = jnp.zeros(acc_ref.shape[1:], jnp.float32)
        return tuple(ms), tuple(cms)

    def stage(il, i, t, slot, carry):
        a_pend, ms, cms, j_pend = carry
        nxt = jnp.minimum(t + 1, nb - 1)
        for g in heads:
            acc_ref[g] = a_pend[g] * acc_ref[g] + pv(g, j_pend, p_ref[g])
        new_cms = []
        for g in heads:
            s_next = scores(g, il, nxt)
            s_ref[1 - slot, g] = s_next
            new_cms.append(jnp.max(s_next, axis=0, keepdims=True))
        new_ms, new_as = [], []
        for g in heads:
            cm = cms[g]
            if gated:
                b = bias_ref[g, pl.ds(t, 1), :]
            else:
                b = jnp.where(t < i, 0.0, neg_inf)
            m_new = jnp.maximum(ms[g], cm + b)
            alpha = jnp.exp2(ms[g] - m_new)
            p = jnp.exp2(s_ref[slot, g] - (m_new - b))
            new_ms.append(m_new)
            new_as.append(alpha)
            p_ref[g] = p.astype(MXU_DTYPE)
        return tuple(new_as), tuple(new_ms), tuple(new_cms), t

    for g in heads:
        acc_ref[g] = jnp.ones(acc_ref.shape[1:], jnp.float32)
        p_ref[g] = jnp.zeros(p_ref.shape[1:], MXU_DTYPE)

    def tile(il, carry):
        a_pend, j_pend = carry
        finish(jnp.maximum(il - 1, 0), a_pend, j_pend)
        cur = jnp.minimum(il, tiles - 1)
        i = base + cur
        ms, cms = start(cur, i)

        def body(u, c):
            return stage(cur, i, 2 * u + 1, 1, stage(cur, i, 2 * u, 0, c))

        trips = jnp.where(il < tiles, (i + 1) // 2, 0)
        a_pend, _, _, j_pend = lax.fori_loop(0, trips, body, (ones, ms, cms, i))
        return a_pend, j_pend

    lax.fori_loop(0, tiles + 1, tile, (ones, jnp.int32(0)))


def _attention(qT, k, vT, kmean, tiles):
    B, H, nb, d, T = qT.shape
    S = nb * T
    dv = HEAD_DIM
    G = ATTN_HEADS
    gated = kmean is not None
    in_specs = [pl.BlockSpec((1, G, tiles, d, T), lambda b, h, i: (b, h, i, 0, 0)),
                pl.BlockSpec((1, G, S, d), lambda b, h, i: (b, h, 0, 0)),
                pl.BlockSpec((1, G, nb, VT_ROWS, T), lambda b, h, i: (b, h, 0, 0, 0))]
    args = [qT, k, vT]
    scratch = [pltpu.VMEM((G, VT_ROWS, T), jnp.float32),
               pltpu.VMEM((2, G, T, T), jnp.float32),
               pltpu.VMEM((G, T, T), MXU_DTYPE)]
    if gated:
        in_specs.append(pl.BlockSpec((1, G, nb, d), lambda b, h, i: (b, h, 0, 0)))
        args.append(kmean)
        scratch.append(pltpu.VMEM((G, nb, T), jnp.float32))
    return pl.pallas_call(
        functools.partial(_attn_kernel, gated=gated),
        grid=(B, H // G, nb // tiles),
        in_specs=in_specs,
        out_specs=pl.BlockSpec((1, tiles * T, G * dv), lambda b, h, i: (b, i, h)),
        out_shape=jax.ShapeDtypeStruct((B, S, H * dv), MXU_DTYPE),
        scratch_shapes=scratch,
        compiler_params=_params("parallel", "parallel", "parallel"),
        name="moba_attention" if gated else "mla_attention",
    )(*args)


def _layer_norm(y, g, b):
    mu = jnp.mean(y, axis=-1, keepdims=True)
    yc = y - mu
    var = jnp.mean(jnp.square(yc), axis=-1, keepdims=True)
    return yc * lax.rsqrt(var + LN_EPS) * g + b


def _outproj_kernel(a1_ref, a2_ref, w_ref, x_ref, g_ref, b_ref, o_ref):
    step = OUT_CHUNK
    k1 = a1_ref.shape[1]
    for r0 in range(0, x_ref.shape[0], step):
        r = slice(r0, r0 + step)
        mix = _dot(a1_ref[r, :], w_ref[0:k1, :]) + _dot(a2_ref[r, :], w_ref[k1:, :])
        o_ref[r, :] = _layer_norm(DEEPNORM_ALPHA * x_ref[r, :] + mix, g_ref[...], b_ref[...])


def _outproj_ln(a1, a2, w, x, g, b):
    M, D = x.shape
    rows = min(OUT_ROWS, M)
    K = a1.shape[1]
    full = lambda a: pl.BlockSpec(a.shape, lambda i: (0,) * a.ndim)
    return pl.pallas_call(
        _outproj_kernel,
        grid=(M // rows,),
        in_specs=[pl.BlockSpec((rows, K), lambda i: (i, 0)), pl.BlockSpec((rows, K), lambda i: (i, 0)),
                  full(w), pl.BlockSpec((rows, D), lambda i: (i, 0)), full(g), full(b)],
        out_specs=pl.BlockSpec((rows, D), lambda i: (i, 0)),
        out_shape=jax.ShapeDtypeStruct((M, D), jnp.float32),
        compiler_params=_params("parallel"),
        name="outproj_ln1",
    )(a1, a2, w, x, g, b)


def _ffn_kernel(x_ref, wup_ref, wdn_ref, g_ref, b_ref, o_ref, xb_ref, acc_ref):
    f = pl.program_id(1)
    last = pl.num_programs(1) - 1

    def partial(r):
        u = jnp.maximum(_dot(xb_ref[r, :], wup_ref[...]), 0.0)
        return _dot(jnp.square(u).astype(MXU_DTYPE), wdn_ref[...])

    @pl.when(f == 0)
    def _():
        xb_ref[...] = x_ref[...].astype(MXU_DTYPE)
        acc_ref[...] = partial(slice(None))

    @pl.when((f > 0) & (f < last))
    def _():
        acc_ref[...] += partial(slice(None))

    @pl.when(f == last)
    def _():
        for r0 in range(0, x_ref.shape[0], FFN_CHUNK):
            r = slice(r0, r0 + FFN_CHUNK)
            y = acc_ref[r, :] + partial(r)
            o_ref[r, :] = _layer_norm(DEEPNORM_ALPHA * x_ref[r, :] + y, g_ref[...], b_ref[...])


def _ffn_ln(x, wup, wdn, g, b):
    M, D = x.shape
    F = wup.shape[1]
    rows = min(FFN_ROWS, M)
    cols = min(FFN_COLS, F)
    vec = pl.BlockSpec((1, D), lambda i, f: (0, 0))
    return pl.pallas_call(
        _ffn_kernel,
        grid=(M // rows, F // cols),
        in_specs=[pl.BlockSpec((rows, D), lambda i, f: (i, 0)),
                  pl.BlockSpec((D, cols), lambda i, f: (0, f)),
                  pl.BlockSpec((cols, D), lambda i, f: (f, 0)),
                  vec, vec],
        out_specs=pl.BlockSpec((rows, D), lambda i, f: (i, 0)),
        out_shape=jax.ShapeDtypeStruct((M, D), jnp.float32),
        scratch_shapes=[pltpu.VMEM((rows, D), MXU_DTYPE), pltpu.VMEM((rows, D), jnp.float32)],
        compiler_params=_params("parallel", "arbitrary"),
        name="ffn_ln2",
    )(x, wup, wdn, g, b)


def _layer_weights(w_in, w_uq, w_ukv, w_out, w_up, w_down):
    D = w_in.shape[0]
    half = MLA_ROPE_DIM // 2
    swap = np.concatenate([np.arange(half, MLA_ROPE_DIM), np.arange(half)])
    c0 = MLA_Q_RANK + MLA_KV_RANK
    c1 = c0 + MLA_ROPE_DIM
    mw = MOBA_HEADS * HEAD_DIM
    cast = lambda w: w.astype(MXU_DTYPE)
    w_in, w_uq, w_ukv = cast(w_in), cast(w_uq), cast(w_ukv)
    w_lat = jnp.concatenate([w_in[:, :c1], w_in[:, c0:c1][:, swap]], axis=1)
    k_cols = w_in[:, c1 + mw:c1 + 2 * mw].reshape(D, MOBA_HEADS, HEAD_DIM)[:, :, MOBA_HEAD_PERM].reshape(D, mw)
    w_m = jnp.concatenate([w_in[:, c1:c1 + mw], k_cols, w_in[:, c1 + 2 * mw:c1 + 3 * mw]], axis=1)
    uq = w_uq.reshape(MLA_Q_RANK, MLA_HEADS, HEAD_DIM + MLA_ROPE_DIM)
    rope_cols = uq[:, :, HEAD_DIM:]
    uq = jnp.concatenate([uq, rope_cols[:, :, swap]], axis=-1).reshape(MLA_Q_RANK, MLA_HEADS * 2 * HEAD_DIM)
    ukv = w_ukv.reshape(MLA_KV_RANK, MLA_HEADS, 2, HEAD_DIM).transpose(0, 2, 1, 3).reshape(MLA_KV_RANK, -1)
    return dict(w_lat=w_lat, w_m=w_m, uq=uq, ukv=ukv, wo=cast(w_out), wup=cast(w_up), wdn=cast(w_down))


def kernel(x, positions, w_in, mla_q_norm, mla_kv_norm, w_uq, w_ukv, w_out, ln1_g, ln1_b, w_up, w_down, ln2_g, ln2_b):
    B, S, D = x.shape
    assert S % ATTN_TILE == 0 and w_in.shape[0] == DEPTH
    for l in range(DEPTH):
        w = _layer_weights(w_in[l], w_uq[l], w_ukv[l], w_out[l], w_up[l], w_down[l])
        row = lambda v: v[l].reshape(1, -1)
        qT_a, k_a, vT_a = _mla_prep(x, positions, w["w_lat"], row(mla_q_norm), row(mla_kv_norm), w["uq"], w["ukv"])
        qT_b, k_b, vT_b, kmean = _moba_prep(x, positions, w["w_m"])
        nb = S // ATTN_TILE
        out_a = _attention(qT_a, k_a, vT_a, None, min(MLA_TILES, nb))
        out_b = _attention(qT_b, k_b, vT_b, kmean, min(MOBA_TILES, nb))
        x1 = _outproj_ln(out_a.reshape(B * S, -1), out_b.reshape(B * S, -1), w["wo"],
                         x.reshape(B * S, D), row(ln1_g), row(ln1_b))
        x = _ffn_ln(x1, w["wup"], w["wdn"], row(ln2_g), row(ln2_b)).reshape(B, S, D)
    return x
```

```python
import functools
import math

import numpy as np
import jax
import jax.numpy as jnp
from jax import lax
from jax.experimental import pallas as pl
from jax.experimental.pallas import tpu as pltpu

HEAD_DIM = 128
MLA_HEADS = 8
MOBA_HEADS = 8
MLA_Q_RANK = 384
MLA_KV_RANK = 256
MLA_ROPE_DIM = 64
MOBA_BLOCK = 256
MOBA_TOPK = 3
ROPE_THETA = 500000.0
PARTIAL_ROT_DIM = 32
LN_EPS = 1e-5
RMS_EPS = 1e-6
DEPTH = 1
DEEPNORM_ALPHA = (2 * DEPTH) ** 0.25
MLA_SCALE = 1.0 / math.sqrt(HEAD_DIM + MLA_ROPE_DIM)
MOBA_SCALE = 1.0 / math.sqrt(HEAD_DIM)
LOG2E = math.log2(math.e)

LANES = 128
MXU_DTYPE = jnp.bfloat16
VMEM_LIMIT_BYTES = 56 * 1024 * 1024

ATTN_TILE = MOBA_BLOCK
ATTN_HEADS = 4
MLA_TILES = 8
MOBA_TILES = 16
BF16_SUBLANES = 16
VT_ROWS = HEAD_DIM + BF16_SUBLANES
MLA_PREP_ROWS = 1024
MLA_CHAIN_ROWS = 512
MOBA_PREP_ROWS = 512
OUT_ROWS = 512
OUT_CHUNK = 256
FFN_ROWS = 512
FFN_COLS = 1024
FFN_CHUNK = 256


def _dot(a, b):
    return jnp.dot(a, b, preferred_element_type=jnp.float32)


def _params(*sem):
    return pltpu.CompilerParams(dimension_semantics=sem, vmem_limit_bytes=VMEM_LIMIT_BYTES)


def _inv_freq(dim):
    return ROPE_THETA ** (-jnp.arange(dim // 2, dtype=jnp.float32) * (2.0 / dim))


def _rms(x, g):
    y = x * lax.rsqrt(jnp.mean(jnp.square(x), axis=-1, keepdims=True) + RMS_EPS)
    return y * g


def _store_vT(vT_ref, h, blk, v):
    vT_ref[0, h, blk, 0:HEAD_DIM, :] = v.T.astype(MXU_DTYPE)
    row = lax.broadcasted_iota(jnp.int32, (BF16_SUBLANES, ATTN_TILE), 0)
    vT_ref[0, h, blk, HEAD_DIM:VT_ROWS, :] = jnp.where(row == 0, 1.0, 0.0).astype(MXU_DTYPE)


def _mla_prep_kernel(x_ref, pos_ref, invf_ref, wlat_ref, gq_ref, gkv_ref, wuq_ref, wukv_ref,
                     qT_ref, k_ref, vT_ref):
    rows = x_ref.shape[1]
    T = ATTN_TILE
    C = min(MLA_CHAIN_ROWS, rows)
    dq = 2 * HEAD_DIM
    xb = x_ref[0].astype(MXU_DTYPE)
    lats = [_dot(xb[c0:c0 + C], wlat_ref[...]) for c0 in range(0, rows, C)]
    for ci, c0 in enumerate(range(0, rows, C)):
        r = slice(c0, c0 + C)
        lat = lats[ci]
        cq = lat[:, :MLA_Q_RANK]
        ckv = lat[:, MLA_Q_RANK:MLA_Q_RANK + MLA_KV_RANK]
        kr2 = lat[:, MLA_Q_RANK + MLA_KV_RANK:]
        qall = _dot(_rms(cq, gq_ref[...]).astype(MXU_DTYPE), wuq_ref[...])
        kvall = _dot(_rms(ckv, gkv_ref[...]).astype(MXU_DTYPE), wukv_ref[...])
        ang = pos_ref[0, r, :].astype(jnp.float32) * invf_ref[...]
        lane = lax.broadcasted_iota(jnp.int32, ang.shape, 1)
        cos, sin = jnp.cos(ang), jnp.sin(ang)
        tab = jnp.where(lane < MLA_ROPE_DIM, cos, jnp.where(lane < MLA_ROPE_DIM + MLA_ROPE_DIM // 2, -sin, sin))
        t = kr2 * tab
        k_rope = jnp.where(lane < MLA_ROPE_DIM, t + pltpu.roll(t, MLA_ROPE_DIM, 1), 0.0).astype(MXU_DTYPE)
        qall = qall * (MLA_SCALE * LOG2E)
        for h in range(MLA_HEADS):
            nope = qall[:, h * dq:h * dq + HEAD_DIM]
            t = qall[:, h * dq + HEAD_DIM:(h + 1) * dq] * tab
            rope = t + pltpu.roll(t, MLA_ROPE_DIM, 1)
            k_ref[0, h, r, 0:HEAD_DIM] = kvall[:, h * HEAD_DIM:(h + 1) * HEAD_DIM].astype(MXU_DTYPE)
            k_ref[0, h, r, HEAD_DIM:dq] = k_rope
            v = kvall[:, (MLA_HEADS + h) * HEAD_DIM:(MLA_HEADS + h + 1) * HEAD_DIM]
            for t0 in range(0, C, T):
                blk = (c0 + t0) // T
                qT_ref[0, h, blk, 0:HEAD_DIM, :] = nope[t0:t0 + T].T.astype(MXU_DTYPE)
                qT_ref[0, h, blk, HEAD_DIM:dq, :] = rope[t0:t0 + T].T.astype(MXU_DTYPE)
                _store_vT(vT_ref, h, blk, v[t0:t0 + T])


def _mla_prep(x, positions, w_lat, gq, gkv, wuq, wukv):
    B, S, D = x.shape
    rows = min(MLA_PREP_ROWS, S)
    nb = S // ATTN_TILE
    H, dq = MLA_HEADS, 2 * HEAD_DIM
    invf = jnp.tile(_inv_freq(MLA_ROPE_DIM), 2 * LANES // MLA_ROPE_DIM).reshape(1, LANES)
    full = lambda a: pl.BlockSpec(a.shape, lambda b, i: (0,) * a.ndim)
    return pl.pallas_call(
        _mla_prep_kernel,
        grid=(B, S // rows),
        in_specs=[pl.BlockSpec((1, rows, D), lambda b, i: (b, i, 0)),
                  pl.BlockSpec((1, rows, 1), lambda b, i: (b, i, 0)),
                  full(invf), full(w_lat), full(gq), full(gkv), full(wuq), full(wukv)],
        out_specs=[pl.BlockSpec((1, H, rows // ATTN_TILE, dq, ATTN_TILE), lambda b, i: (b, 0, i, 0, 0)),
                   pl.BlockSpec((1, H, rows, dq), lambda b, i: (b, 0, i, 0)),
                   pl.BlockSpec((1, H, rows // ATTN_TILE, VT_ROWS, ATTN_TILE), lambda b, i: (b, 0, i, 0, 0))],
        out_shape=[jax.ShapeDtypeStruct((B, H, nb, dq, ATTN_TILE), MXU_DTYPE),
                   jax.ShapeDtypeStruct((B, H, S, dq), MXU_DTYPE),
                   jax.ShapeDtypeStruct((B, H, nb, VT_ROWS, ATTN_TILE), MXU_DTYPE)],
        compiler_params=_params("parallel", "parallel"),
        name="mla_prep",
    )(x, positions.reshape(B, S, 1), invf, w_lat, gq, gkv, wuq, wukv)


ROT_HALF = PARTIAL_ROT_DIM // 2
ROT_X2 = LANES // 2
MOBA_HEAD_PERM = np.concatenate([np.arange(0, ROT_HALF), np.arange(PARTIAL_ROT_DIM, PARTIAL_ROT_DIM + ROT_X2 - ROT_HALF),
                                 np.arange(ROT_HALF, PARTIAL_ROT_DIM),
                                 np.arange(PARTIAL_ROT_DIM + ROT_X2 - ROT_HALF, HEAD_DIM)])


def _moba_prep_kernel(x_ref, w_ref, posc_ref, posr_ref, invr_ref, invc_ref, qT_ref, k_ref, vT_ref, kmean_ref):
    i = pl.program_id(1)
    rows = x_ref.shape[1]
    nblk = rows // ATTN_TILE
    T = ATTN_TILE
    xb = x_ref[0].astype(MXU_DTYPE)
    ang = posc_ref[0].astype(jnp.float32) * invr_ref[...]
    lane = lax.broadcasted_iota(jnp.int32, ang.shape, 1)
    k_cos = jnp.cos(ang)
    k_sin = jnp.where(lane < ROT_X2, -jnp.sin(ang), jnp.sin(ang))
    ang_t = invc_ref[...] * posr_ref[0].astype(jnp.float32)
    q_cos, q_sin = jnp.cos(ang_t), jnp.sin(ang_t)
    q_scale = MOBA_SCALE * LOG2E
    width = 2 * HEAD_DIM
    per_kind = MOBA_HEADS * HEAD_DIM // width
    for chunk in range(3 * per_kind):
        hm = _dot(xb, w_ref[:, chunk * width:(chunk + 1) * width])
        kind, pair = divmod(chunk, per_kind)
        for hh in range(2):
            h = 2 * pair + hh
            xh = hm[:, hh * HEAD_DIM:(hh + 1) * HEAD_DIM]
            if kind == 0:
                for blk in range(nblk):
                    t = xh[blk * T:(blk + 1) * T].T * q_scale
                    c, s = q_cos[:, blk * T:(blk + 1) * T], q_sin[:, blk * T:(blk + 1) * T]
                    x1, x2 = t[0:ROT_HALF], t[ROT_HALF:PARTIAL_ROT_DIM]
                    split = PARTIAL_ROT_DIM + ROT_X2 - ROT_HALF
                    qT_ref[0, h, blk, 0:ROT_HALF, :] = (x1 * c - x2 * s).astype(MXU_DTYPE)
                    qT_ref[0, h, blk, ROT_HALF:ROT_X2, :] = t[PARTIAL_ROT_DIM:split].astype(MXU_DTYPE)
                    qT_ref[0, h, blk, ROT_X2:ROT_X2 + ROT_HALF, :] = (x2 * c + x1 * s).astype(MXU_DTYPE)
                    qT_ref[0, h, blk, ROT_X2 + ROT_HALF:HEAD_DIM, :] = t[split:].astype(MXU_DTYPE)
            elif kind == 1:
                k = xh * k_cos + pltpu.roll(xh, ROT_X2, 1) * k_sin
                k_ref[0, h] = k.astype(MXU_DTYPE)
                for blk in range(nblk):
                    mean = jnp.mean(k[blk * MOBA_BLOCK:(blk + 1) * MOBA_BLOCK], axis=0, keepdims=True)
                    kmean_ref[0, h, pl.ds(i * nblk + blk, 1), :] = mean
            else:
                for blk in range(nblk):
                    _store_vT(vT_ref, h, blk, xh[blk * T:(blk + 1) * T])


def _moba_prep(x, positions, w_m):
    B, S, D = x.shape
    rows = min(MOBA_PREP_ROWS, S)
    nb = S // ATTN_TILE
    H = MOBA_HEADS
    invf = _inv_freq(PARTIAL_ROT_DIM)
    gap = jnp.zeros((ROT_X2 - ROT_HALF,), jnp.float32)
    invr = jnp.concatenate([invf, gap, invf, gap]).reshape(1, LANES)
    invc = invf.reshape(ROT_HALF, 1)
    full = lambda a: pl.BlockSpec(a.shape, lambda b, i: (0,) * a.ndim)
    tile_t = lambda d: pl.BlockSpec((1, H, rows // ATTN_TILE, d, ATTN_TILE), lambda b, i: (b, 0, i, 0, 0))
    shape_t = lambda d: jax.ShapeDtypeStruct((B, H, nb, d, ATTN_TILE), MXU_DTYPE)
    return pl.pallas_call(
        _moba_prep_kernel,
        grid=(B, S // rows),
        in_specs=[pl.BlockSpec((1, rows, D), lambda b, i: (b, i, 0)),
                  full(w_m),
                  pl.BlockSpec((1, rows, 1), lambda b, i: (b, i, 0)),
                  pl.BlockSpec((1, 1, rows), lambda b, i: (b, 0, i)),
                  full(invr), full(invc)],
        out_specs=[tile_t(HEAD_DIM),
                   pl.BlockSpec((1, H, rows, HEAD_DIM), lambda b, i: (b, 0, i, 0)),
                   tile_t(VT_ROWS),
                   pl.BlockSpec((1, H, nb, HEAD_DIM), lambda b, i: (b, 0, 0, 0))],
        out_shape=[shape_t(HEAD_DIM),
                   jax.ShapeDtypeStruct((B, H, S, HEAD_DIM), MXU_DTYPE),
                   shape_t(VT_ROWS),
                   jax.ShapeDtypeStruct((B, H, nb, HEAD_DIM), jnp.float32)],
        compiler_params=_params("parallel", "arbitrary"),
        name="moba_prep",
    )(x, w_m, positions.reshape(B, S, 1), positions.reshape(B, 1, S), invr, invc)


def _attn_kernel(*refs, gated):
    if gated:
        qT_ref, k_ref, vT_ref, kmean_ref, o_ref, acc_ref, s_ref, p_ref, bias_ref = refs
    else:
        qT_ref, k_ref, vT_ref, o_ref, acc_ref, s_ref, p_ref = refs
    T = ATTN_TILE
    G, tiles = qT_ref.shape[1], qT_ref.shape[2]
    nb, dv = vT_ref.shape[2], HEAD_DIM
    base = pl.program_id(2) * tiles
    neg_inf = -jnp.inf
    heads = range(G)
    ones = tuple(jnp.ones((1, T), jnp.float32) for _ in heads)

    def scores(g, il, j):
        kb = k_ref[0, g, pl.ds(pl.multiple_of(j * T, T), T), :]
        return _dot(kb, qT_ref[0, g, il])

    def pv(g, j, p):
        return _dot(vT_ref[0, g, j], p)

    def finish(il, a_pend, j_pend):
        accs = [a_pend[g] * acc_ref[g] + pv(g, j_pend, p_ref[g]) for g in heads]
        rows = pl.ds(pl.multiple_of(il * T, T), T)
        for g in heads:
            o_ref[0, rows, g * dv:(g + 1) * dv] = (accs[g][0:dv] / accs[g][dv:dv + 1]).T.astype(o_ref.dtype)

    def start(il, i):
        if gated:
            gates = [_dot(kmean_ref[0, g].astype(MXU_DTYPE), qT_ref[0, g, il]) for g in heads]
        s_own = [scores(g, il, i) for g in heads]
        cms = []
        for g in heads:
            s_first = scores(g, il, 0)
            s_ref[0, g] = s_first
            cms.append(jnp.max(s_first, axis=0, keepdims=True))
        if gated:
            for g in heads:
                row = lax.broadcasted_iota(jnp.int32, gates[g].shape, 0)
                past = row < i
                gate = jnp.where(past, gates[g], neg_inf)
                taken = row < 0
                for _ in range(min(MOBA_TOPK, nb)):
                    cand = jnp.where(taken, neg_inf, gate)
                    best = jnp.max(cand, axis=0, keepdims=True)
                    hit = (cand == best) & jnp.logical_not(taken)
                    first = jnp.min(jnp.where(hit, row, nb), axis=0, keepdims=True)
                    taken = taken | (row == first)
                bias_ref[g] = jnp.where(taken & past, 0.0, neg_inf)
        kpos = lax.broadcasted_iota(jnp.int32, (T, T), 0)
        qpos = lax.broadcasted_iota(jnp.int32, (T, T), 1)
        causal = kpos <= qpos
        ms = []
        for g in heads:
            s = jnp.where(causal, s_own[g], neg_inf)
            m = jnp.max(s, axis=0, keepdims=True)
            ms.append(m)
            p_ref[g] = jnp.exp2(s - m).astype(MXU_DTYPE)
            acc_ref[g] = jnp.zeros(acc_ref.shape[1:], jnp.float32)
        return tuple(ms), tuple(cms)

    def stage(il, i, t, slot, carry):
        a_pend, ms, cms, j_pend = carry
        nxt = jnp.minimum(t + 1, nb - 1)
        for g in heads:
            acc_ref[g] = a_pend[g] * acc_ref[g] + pv(g, j_pend, p_ref[g])
        new_cms = []
        for g in heads:
            s_next = scores(g, il, nxt)
            s_ref[1 - slot, g] = s_next
            new_cms.append(jnp.max(s_next, axis=0, keepdims=True))
        new_ms, new_as = [], []
        for g in heads:
            cm = cms[g]
            if gated:
                b = bias_ref[g, pl.ds(t, 1), :]
            else:
                b = jnp.where(t < i, 0.0, neg_inf)
            m_new = jnp.maximum(ms[g], cm + b)
            alpha = jnp.exp2(ms[g] - m_new)
            p = jnp.exp2(s_ref[slot, g] - (m_new - b))
            new_ms.append(m_new)
            new_as.append(alpha)
            p_ref[g] = p.astype(MXU_DTYPE)
        return tuple(new_as), tuple(new_ms), tuple(new_cms), t

    for g in heads:
        acc_ref[g] = jnp.ones(acc_ref.shape[1:], jnp.float32)
        p_ref[g] = jnp.zeros(p_ref.shape[1:], MXU_DTYPE)

    def tile(il, carry):
        a_pend, j_pend = carry
        finish(jnp.maximum(il - 1, 0), a_pend, j_pend)
        cur = jnp.minimum(il, tiles - 1)
        i = base + cur
        ms, cms = start(cur, i)

        def pair(t0, c):
            return stage(cur, i, t0 + 1, 1, stage(cur, i, t0, 0, c))

        pairs = jnp.where(il < tiles, (i + 1) // 2, 0)
        c = lax.fori_loop(0, pairs // 2, lambda u, c: pair(4 * u + 2, pair(4 * u, c)), (ones, ms, cms, i))
        a_pend, _, _, j_pend = lax.fori_loop(0, pairs % 2, lambda u, c: pair(4 * (pairs // 2), c), c)
        return a_pend, j_pend

    lax.fori_loop(0, tiles + 1, tile, (ones, jnp.int32(0)))


def _attention(qT, k, vT, kmean, tiles):
    B, H, nb, d, T = qT.shape
    S = nb * T
    dv = HEAD_DIM
    G = ATTN_HEADS
    gated = kmean is not None
    in_specs = [pl.BlockSpec((1, G, tiles, d, T), lambda b, h, i: (b, h, i, 0, 0)),
                pl.BlockSpec((1, G, S, d), lambda b, h, i: (b, h, 0, 0)),
                pl.BlockSpec((1, G, nb, VT_ROWS, T), lambda b, h, i: (b, h, 0, 0, 0))]
    args = [qT, k, vT]
    scratch = [pltpu.VMEM((G, VT_ROWS, T), jnp.float32),
               pltpu.VMEM((2, G, T, T), jnp.float32),
               pltpu.VMEM((G, T, T), MXU_DTYPE)]
    if gated:
        in_specs.append(pl.BlockSpec((1, G, nb, d), lambda b, h, i: (b, h, 0, 0)))
        args.append(kmean)
        scratch.append(pltpu.VMEM((G, nb, T), jnp.float32))
    return pl.pallas_call(
        functools.partial(_attn_kernel, gated=gated),
        grid=(B, H // G, nb // tiles),
        in_specs=in_specs,
        out_specs=pl.BlockSpec((1, tiles * T, G * dv), lambda b, h, i: (b, i, h)),
        out_shape=jax.ShapeDtypeStruct((B, S, H * dv), MXU_DTYPE),
        scratch_shapes=scratch,
        compiler_params=_params("parallel", "parallel", "parallel"),
        name="moba_attention" if gated else "mla_attention",
    )(*args)


def _layer_norm(y, g, b):
    mu = jnp.mean(y, axis=-1, keepdims=True)
    yc = y - mu
    var = jnp.mean(jnp.square(yc), axis=-1, keepdims=True)
    return yc * lax.rsqrt(var + LN_EPS) * g + b


def _outproj_kernel(a1_ref, a2_ref, w_ref, x_ref, g_ref, b_ref, o_ref):
    step = OUT_CHUNK
    k1 = a1_ref.shape[1]
    for r0 in range(0, x_ref.shape[0], step):
        r = slice(r0, r0 + step)
        mix = _dot(a1_ref[r, :], w_ref[0:k1, :]) + _dot(a2_ref[r, :], w_ref[k1:, :])
        o_ref[r, :] = _layer_norm(DEEPNORM_ALPHA * x_ref[r, :] + mix, g_ref[...], b_ref[...])


def _outproj_ln(a1, a2, w, x, g, b):
    M, D = x.shape
    rows = min(OUT_ROWS, M)
    K = a1.shape[1]
    full = lambda a: pl.BlockSpec(a.shape, lambda i: (0,) * a.ndim)
    return pl.pallas_call(
        _outproj_kernel,
        grid=(M // rows,),
        in_specs=[pl.BlockSpec((rows, K), lambda i: (i, 0)), pl.BlockSpec((rows, K), lambda i: (i, 0)),
                  full(w), pl.BlockSpec((rows, D), lambda i: (i, 0)), full(g), full(b)],
        out_specs=pl.BlockSpec((rows, D), lambda i: (i, 0)),
        out_shape=jax.ShapeDtypeStruct((M, D), jnp.float32),
        compiler_params=_params("parallel"),
        name="outproj_ln1",
    )(a1, a2, w, x, g, b)


def _ffn_kernel(x_ref, wup_ref, wdn_ref, g_ref, b_ref, o_ref, xb_ref, acc_ref):
    f = pl.program_id(1)
    last = pl.num_programs(1) - 1

    def partial(r):
        u = jnp.maximum(_dot(xb_ref[r, :], wup_ref[...]), 0.0)
        return _dot(jnp.square(u).astype(MXU_DTYPE), wdn_ref[...])

    @pl.when(f == 0)
    def _():
        xb_ref[...] = x_ref[...].astype(MXU_DTYPE)
        acc_ref[...] = partial(slice(None))

    @pl.when((f > 0) & (f < last))
    def _():
        acc_ref[...] += partial(slice(None))

    @pl.when(f == last)
    def _():
        for r0 in range(0, x_ref.shape[0], FFN_CHUNK):
            r = slice(r0, r0 + FFN_CHUNK)
            y = acc_ref[r, :] + partial(r)
            o_ref[r, :] = _layer_norm(DEEPNORM_ALPHA * x_ref[r, :] + y, g_ref[...], b_ref[...])


def _ffn_ln(x, wup, wdn, g, b):
    M, D = x.shape
    F = wup.shape[1]
    rows = min(FFN_ROWS, M)
    cols = min(FFN_COLS, F)
    vec = pl.BlockSpec((1, D), lambda i, f: (0, 0))
    return pl.pallas_call(
        _ffn_kernel,
        grid=(M // rows, F // cols),
        in_specs=[pl.BlockSpec((rows, D), lambda i, f: (i, 0)),
                  pl.BlockSpec((D, cols), lambda i, f: (0, f)),
                  pl.BlockSpec((cols, D), lambda i, f: (f, 0)),
                  vec, vec],
        out_specs=pl.BlockSpec((rows, D), lambda i, f: (i, 0)),
        out_shape=jax.ShapeDtypeStruct((M, D), jnp.float32),
        scratch_shapes=[pltpu.VMEM((rows, D), MXU_DTYPE), pltpu.VMEM((rows, D), jnp.float32)],
        compiler_params=_params("parallel", "arbitrary"),
        name="ffn_ln2",
    )(x, wup, wdn, g, b)


def _layer_weights(w_in, w_uq, w_ukv, w_out, w_up, w_down):
    D = w_in.shape[0]
    half = MLA_ROPE_DIM // 2
    swap = np.concatenate([np.arange(half, MLA_ROPE_DIM), np.arange(half)])
    c0 = MLA_Q_RANK + MLA_KV_RANK
    c1 = c0 + MLA_ROPE_DIM
    mw = MOBA_HEADS * HEAD_DIM
    cast = lambda w: w.astype(MXU_DTYPE)
    w_in, w_uq, w_ukv = cast(w_in), cast(w_uq), cast(w_ukv)
    w_lat = jnp.concatenate([w_in[:, :c1], w_in[:, c0:c1][:, swap]], axis=1)
    k_cols = w_in[:, c1 + mw:c1 + 2 * mw].reshape(D, MOBA_HEADS, HEAD_DIM)[:, :, MOBA_HEAD_PERM].reshape(D, mw)
    w_m = jnp.concatenate([w_in[:, c1:c1 + mw], k_cols, w_in[:, c1 + 2 * mw:c1 + 3 * mw]], axis=1)
    uq = w_uq.reshape(MLA_Q_RANK, MLA_HEADS, HEAD_DIM + MLA_ROPE_DIM)
    rope_cols = uq[:, :, HEAD_DIM:]
    uq = jnp.concatenate([uq, rope_cols[:, :, swap]], axis=-1).reshape(MLA_Q_RANK, MLA_HEADS * 2 * HEAD_DIM)
    ukv = w_ukv.reshape(MLA_KV_RANK, MLA_HEADS, 2, HEAD_DIM).transpose(0, 2, 1, 3).reshape(MLA_KV_RANK, -1)
    return dict(w_lat=w_lat, w_m=w_m, uq=uq, ukv=ukv, wo=cast(w_out), wup=cast(w_up), wdn=cast(w_down))


def kernel(x, positions, w_in, mla_q_norm, mla_kv_norm, w_uq, w_ukv, w_out, ln1_g, ln1_b, w_up, w_down, ln2_g, ln2_b):
    B, S, D = x.shape
    assert S % ATTN_TILE == 0 and w_in.shape[0] == DEPTH
    for l in range(DEPTH):
        w = _layer_weights(w_in[l], w_uq[l], w_ukv[l], w_out[l], w_up[l], w_down[l])
        row = lambda v: v[l].reshape(1, -1)
        qT_a, k_a, vT_a = _mla_prep(x, positions, w["w_lat"], row(mla_q_norm), row(mla_kv_norm), w["uq"], w["ukv"])
        qT_b, k_b, vT_b, kmean = _moba_prep(x, positions, w["w_m"])
        nb = S // ATTN_TILE
        out_a = _attention(qT_a, k_a, vT_a, None, min(MLA_TILES, nb))
        out_b = _attention(qT_b, k_b, vT_b, kmean, min(MOBA_TILES, nb))
        x1 = _outproj_ln(out_a.reshape(B * S, -1), out_b.reshape(B * S, -1), w["wo"],
                         x.reshape(B * S, D), row(ln1_g), row(ln1_b))
        x = _ffn_ln(x1, w["wup"], w["wdn"], row(ln2_g), row(ln2_b)).reshape(B, S, D)
    return x
```

```python
import functools
import math

import numpy as np
import jax
import jax.numpy as jnp
from jax import lax
from jax.experimental import pallas as pl
from jax.experimental.pallas import tpu as pltpu

HEAD_DIM = 128
MLA_HEADS = 8
MOBA_HEADS = 8
MLA_Q_RANK = 384
MLA_KV_RANK = 256
MLA_ROPE_DIM = 64
MOBA_BLOCK = 256
MOBA_TOPK = 3
ROPE_THETA = 500000.0
PARTIAL_ROT_DIM = 32
LN_EPS = 1e-5
RMS_EPS = 1e-6
DEPTH = 1
DEEPNORM_ALPHA = (2 * DEPTH) ** 0.25
MLA_SCALE = 1.0 / math.sqrt(HEAD_DIM + MLA_ROPE_DIM)
MOBA_SCALE = 1.0 / math.sqrt(HEAD_DIM)
LOG2E = math.log2(math.e)

LANES = 128
MXU_DTYPE = jnp.bfloat16
VMEM_LIMIT_BYTES = 56 * 1024 * 1024

ATTN_TILE = MOBA_BLOCK
ATTN_HEADS = 4
MLA_TILES = 8
MOBA_TILES = 16
BF16_SUBLANES = 16
VT_ROWS = HEAD_DIM + BF16_SUBLANES
MLA_PREP_ROWS = 1024
MLA_CHAIN_ROWS = 512
MOBA_PREP_ROWS = 1024
OUT_ROWS = 512
OUT_CHUNK = 256
FFN_ROWS = 512
FFN_COLS = 1024
FFN_CHUNK = 256


def _dot(a, b):
    return jnp.dot(a, b, preferred_element_type=jnp.float32)


def _params(*sem):
    return pltpu.CompilerParams(dimension_semantics=sem, vmem_limit_bytes=VMEM_LIMIT_BYTES)


def _inv_freq(dim):
    return ROPE_THETA ** (-jnp.arange(dim // 2, dtype=jnp.float32) * (2.0 / dim))


def _rms(x, g):
    y = x * lax.rsqrt(jnp.mean(jnp.square(x), axis=-1, keepdims=True) + RMS_EPS)
    return y * g


def _store_vT(vT_ref, h, blk, v):
    vT_ref[0, h, blk, 0:HEAD_DIM, :] = v.T.astype(MXU_DTYPE)
    row = lax.broadcasted_iota(jnp.int32, (BF16_SUBLANES, ATTN_TILE), 0)
    vT_ref[0, h, blk, HEAD_DIM:VT_ROWS, :] = jnp.where(row == 0, 1.0, 0.0).astype(MXU_DTYPE)


def _mla_prep_kernel(x_ref, pos_ref, invf_ref, wlat_ref, gq_ref, gkv_ref, wuq_ref, wukv_ref,
                     qT_ref, k_ref, vT_ref):
    rows = x_ref.shape[1]
    T = ATTN_TILE
    C = min(MLA_CHAIN_ROWS, rows)
    dq = 2 * HEAD_DIM
    xb = x_ref[0].astype(MXU_DTYPE)
    lats = [_dot(xb[c0:c0 + C], wlat_ref[...]) for c0 in range(0, rows, C)]
    for ci, c0 in enumerate(range(0, rows, C)):
        r = slice(c0, c0 + C)
        lat = lats[ci]
        cq = lat[:, :MLA_Q_RANK]
        ckv = lat[:, MLA_Q_RANK:MLA_Q_RANK + MLA_KV_RANK]
        kr2 = lat[:, MLA_Q_RANK + MLA_KV_RANK:]
        qall = _dot(_rms(cq, gq_ref[...]).astype(MXU_DTYPE), wuq_ref[...])
        kvall = _dot(_rms(ckv, gkv_ref[...]).astype(MXU_DTYPE), wukv_ref[...])
        ang = pos_ref[0, r, :].astype(jnp.float32) * invf_ref[...]
        lane = lax.broadcasted_iota(jnp.int32, ang.shape, 1)
        cos, sin = jnp.cos(ang), jnp.sin(ang)
        tab = jnp.where(lane < MLA_ROPE_DIM, cos, jnp.where(lane < MLA_ROPE_DIM + MLA_ROPE_DIM // 2, -sin, sin))
        t = kr2 * tab
        k_rope = jnp.where(lane < MLA_ROPE_DIM, t + pltpu.roll(t, MLA_ROPE_DIM, 1), 0.0).astype(MXU_DTYPE)
        qall = qall * (MLA_SCALE * LOG2E)
        for h in range(MLA_HEADS):
            nope = qall[:, h * dq:h * dq + HEAD_DIM]
            t = qall[:, h * dq + HEAD_DIM:(h + 1) * dq] * tab
            rope = t + pltpu.roll(t, MLA_ROPE_DIM, 1)
            k_ref[0, h, r, 0:HEAD_DIM] = kvall[:, h * HEAD_DIM:(h + 1) * HEAD_DIM].astype(MXU_DTYPE)
            k_ref[0, h, r, HEAD_DIM:dq] = k_rope
            v = kvall[:, (MLA_HEADS + h) * HEAD_DIM:(MLA_HEADS + h + 1) * HEAD_DIM]
            for t0 in range(0, C, T):
                blk = (c0 + t0) // T
                qT_ref[0, h, blk, 0:HEAD_DIM, :] = nope[t0:t0 + T].T.astype(MXU_DTYPE)
                qT_ref[0, h, blk, HEAD_DIM:dq, :] = rope[t0:t0 + T].T.astype(MXU_DTYPE)
                _store_vT(vT_ref, h, blk, v[t0:t0 + T])


def _mla_prep(x, positions, w_lat, gq, gkv, wuq, wukv):
    B, S, D = x.shape
    rows = min(MLA_PREP_ROWS, S)
    nb = S // ATTN_TILE
    H, dq = MLA_HEADS, 2 * HEAD_DIM
    invf = jnp.tile(_inv_freq(MLA_ROPE_DIM), 2 * LANES // MLA_ROPE_DIM).reshape(1, LANES)
    full = lambda a: pl.BlockSpec(a.shape, lambda b, i: (0,) * a.ndim)
    return pl.pallas_call(
        _mla_prep_kernel,
        grid=(B, S // rows),
        in_specs=[pl.BlockSpec((1, rows, D), lambda b, i: (b, i, 0)),
                  pl.BlockSpec((1, rows, 1), lambda b, i: (b, i, 0)),
                  full(invf), full(w_lat), full(gq), full(gkv), full(wuq), full(wukv)],
        out_specs=[pl.BlockSpec((1, H, rows // ATTN_TILE, dq, ATTN_TILE), lambda b, i: (b, 0, i, 0, 0)),
                   pl.BlockSpec((1, H, rows, dq), lambda b, i: (b, 0, i, 0)),
                   pl.BlockSpec((1, H, rows // ATTN_TILE, VT_ROWS, ATTN_TILE), lambda b, i: (b, 0, i, 0, 0))],
        out_shape=[jax.ShapeDtypeStruct((B, H, nb, dq, ATTN_TILE), MXU_DTYPE),
                   jax.ShapeDtypeStruct((B, H, S, dq), MXU_DTYPE),
                   jax.ShapeDtypeStruct((B, H, nb, VT_ROWS, ATTN_TILE), MXU_DTYPE)],
        compiler_params=_params("parallel", "parallel"),
        name="mla_prep",
    )(x, positions.reshape(B, S, 1), invf, w_lat, gq, gkv, wuq, wukv)


ROT_HALF = PARTIAL_ROT_DIM // 2
ROT_X2 = LANES // 2
MOBA_HEAD_PERM = np.concatenate([np.arange(0, ROT_HALF), np.arange(PARTIAL_ROT_DIM, PARTIAL_ROT_DIM + ROT_X2 - ROT_HALF),
                                 np.arange(ROT_HALF, PARTIAL_ROT_DIM),
                                 np.arange(PARTIAL_ROT_DIM + ROT_X2 - ROT_HALF, HEAD_DIM)])


def _moba_prep_kernel(x_ref, w_ref, posc_ref, posr_ref, invr_ref, invc_ref, qT_ref, k_ref, vT_ref, kmean_ref):
    i = pl.program_id(1)
    rows = x_ref.shape[1]
    nblk = rows // ATTN_TILE
    T = ATTN_TILE
    xb = x_ref[0].astype(MXU_DTYPE)
    ang = posc_ref[0].astype(jnp.float32) * invr_ref[...]
    lane = lax.broadcasted_iota(jnp.int32, ang.shape, 1)
    k_cos = jnp.cos(ang)
    k_sin = jnp.where(lane < ROT_X2, -jnp.sin(ang), jnp.sin(ang))
    ang_t = invc_ref[...] * posr_ref[0].astype(jnp.float32)
    q_cos, q_sin = jnp.cos(ang_t), jnp.sin(ang_t)
    q_scale = MOBA_SCALE * LOG2E
    width = 2 * HEAD_DIM
    per_kind = MOBA_HEADS * HEAD_DIM // width
    for chunk in range(3 * per_kind):
        hm = _dot(xb, w_ref[:, chunk * width:(chunk + 1) * width])
        kind, pair = divmod(chunk, per_kind)
        for hh in range(2):
            h = 2 * pair + hh
            xh = hm[:, hh * HEAD_DIM:(hh + 1) * HEAD_DIM]
            if kind == 0:
                for blk in range(nblk):
                    t = xh[blk * T:(blk + 1) * T].T * q_scale
                    c, s = q_cos[:, blk * T:(blk + 1) * T], q_sin[:, blk * T:(blk + 1) * T]
                    x1, x2 = t[0:ROT_HALF], t[ROT_HALF:PARTIAL_ROT_DIM]
                    split = PARTIAL_ROT_DIM + ROT_X2 - ROT_HALF
                    qT_ref[0, h, blk, 0:ROT_HALF, :] = (x1 * c - x2 * s).astype(MXU_DTYPE)
                    qT_ref[0, h, blk, ROT_HALF:ROT_X2, :] = t[PARTIAL_ROT_DIM:split].astype(MXU_DTYPE)
                    qT_ref[0, h, blk, ROT_X2:ROT_X2 + ROT_HALF, :] = (x2 * c + x1 * s).astype(MXU_DTYPE)
                    qT_ref[0, h, blk, ROT_X2 + ROT_HALF:HEAD_DIM, :] = t[split:].astype(MXU_DTYPE)
            elif kind == 1:
                k = xh * k_cos + pltpu.roll(xh, ROT_X2, 1) * k_sin
                k_ref[0, h] = k.astype(MXU_DTYPE)
                for blk in range(nblk):
                    mean = jnp.mean(k[blk * MOBA_BLOCK:(blk + 1) * MOBA_BLOCK], axis=0, keepdims=True)
                    kmean_ref[0, h, pl.ds(i * nblk + blk, 1), :] = mean
            else:
                for blk in range(nblk):
                    _store_vT(vT_ref, h, blk, xh[blk * T:(blk + 1) * T])


def _moba_prep(x, positions, w_m):
    B, S, D = x.shape
    rows = min(MOBA_PREP_ROWS, S)
    nb = S // ATTN_TILE
    H = MOBA_HEADS
    invf = _inv_freq(PARTIAL_ROT_DIM)
    gap = jnp.zeros((ROT_X2 - ROT_HALF,), jnp.float32)
    invr = jnp.concatenate([invf, gap, invf, gap]).reshape(1, LANES)
    invc = invf.reshape(ROT_HALF, 1)
    full = lambda a: pl.BlockSpec(a.shape, lambda b, i: (0,) * a.ndim)
    tile_t = lambda d: pl.BlockSpec((1, H, rows // ATTN_TILE, d, ATTN_TILE), lambda b, i: (b, 0, i, 0, 0))
    shape_t = lambda d: jax.ShapeDtypeStruct((B, H, nb, d, ATTN_TILE), MXU_DTYPE)
    return pl.pallas_call(
        _moba_prep_kernel,
        grid=(B, S // rows),
        in_specs=[pl.BlockSpec((1, rows, D), lambda b, i: (b, i, 0)),
                  pl.BlockSpec(w_m.shape, lambda b, i: (0, 0), pipeline_mode=pl.Buffered(1)),
                  pl.BlockSpec((1, rows, 1), lambda b, i: (b, i, 0)),
                  pl.BlockSpec((1, 1, rows), lambda b, i: (b, 0, i)),
                  full(invr), full(invc)],
        out_specs=[tile_t(HEAD_DIM),
                   pl.BlockSpec((1, H, rows, HEAD_DIM), lambda b, i: (b, 0, i, 0)),
                   tile_t(VT_ROWS),
                   pl.BlockSpec((1, H, nb, HEAD_DIM), lambda b, i: (b, 0, 0, 0))],
        out_shape=[shape_t(HEAD_DIM),
                   jax.ShapeDtypeStruct((B, H, S, HEAD_DIM), MXU_DTYPE),
                   shape_t(VT_ROWS),
                   jax.ShapeDtypeStruct((B, H, nb, HEAD_DIM), jnp.float32)],
        compiler_params=_params("parallel", "arbitrary"),
        name="moba_prep",
    )(x, w_m, positions.reshape(B, S, 1), positions.reshape(B, 1, S), invr, invc)


def _attn_kernel(*refs, gated):
    if gated:
        qT_ref, k_ref, vT_ref, kmean_ref, o_ref, acc_ref, s_ref, p_ref, bias_ref = refs
    else:
        qT_ref, k_ref, vT_ref, o_ref, acc_ref, s_ref, p_ref = refs
    T = ATTN_TILE
    G, tiles = qT_ref.shape[1], qT_ref.shape[2]
    nb, dv = vT_ref.shape[2], HEAD_DIM
    base = pl.program_id(2) * tiles
    neg_inf = -jnp.inf
    heads = range(G)
    ones = tuple(jnp.ones((1, T), jnp.float32) for _ in heads)

    def scores(g, il, j):
        kb = k_ref[0, g, pl.ds(pl.multiple_of(j * T, T), T), :]
        return _dot(kb, qT_ref[0, g, il])

    def pv(g, j, p):
        return _dot(vT_ref[0, g, j], p)

    def finish(il, a_pend, j_pend):
        accs = [a_pend[g] * acc_ref[g] + pv(g, j_pend, p_ref[g]) for g in heads]
        rows = pl.ds(pl.multiple_of(il * T, T), T)
        for g in heads:
            inv = 1.0 / accs[g][dv:dv + 1]
            o_ref[0, rows, g * dv:(g + 1) * dv] = (accs[g][0:dv] * inv).T.astype(o_ref.dtype)

    def start(il, i):
        if gated:
            gates = [_dot(kmean_ref[0, g].astype(MXU_DTYPE), qT_ref[0, g, il]) for g in heads]
        s_own = [scores(g, il, i) for g in heads]
        cms = []
        for g in heads:
            s_first = scores(g, il, 0)
            s_ref[0, g] = s_first
            cms.append(jnp.max(s_first, axis=0, keepdims=True))
        if gated:
            for g in heads:
                row = lax.broadcasted_iota(jnp.int32, gates[g].shape, 0)
                past = row < i
                gate = jnp.where(past, gates[g], neg_inf)
                taken = row < 0
                for _ in range(min(MOBA_TOPK, nb)):
                    cand = jnp.where(taken, neg_inf, gate)
                    best = jnp.max(cand, axis=0, keepdims=True)
                    hit = (cand == best) & jnp.logical_not(taken)
                    first = jnp.min(jnp.where(hit, row, nb), axis=0, keepdims=True)
                    taken = taken | (row == first)
                bias_ref[g] = jnp.where(taken & past, 0.0, neg_inf)
        kpos = lax.broadcasted_iota(jnp.int32, (T, T), 0)
        qpos = lax.broadcasted_iota(jnp.int32, (T, T), 1)
        causal = kpos <= qpos
        ms = []
        for g in heads:
            s = jnp.where(causal, s_own[g], neg_inf)
            m = jnp.max(s, axis=0, keepdims=True)
            ms.append(m)
            p_ref[g] = jnp.exp2(s - m).astype(MXU_DTYPE)
            acc_ref[g] = jnp.zeros(acc_ref.shape[1:], jnp.float32)
        return tuple(ms), tuple(cms)

    def stage(il, i, t, slot, carry):
        a_pend, ms, cms, j_pend = carry
        nxt = jnp.minimum(t + 1, nb - 1)
        for g in heads:
            acc_ref[g] = a_pend[g] * acc_ref[g] + pv(g, j_pend, p_ref[g])
        new_cms = []
        for g in heads:
            s_next = scores(g, il, nxt)
            s_ref[1 - slot, g] = s_next
            new_cms.append(jnp.max(s_next, axis=0, keepdims=True))
        new_ms, new_as = [], []
        for g in heads:
            cm = cms[g]
            if gated:
                b = bias_ref[g, pl.ds(t, 1), :]
            else:
                b = jnp.where(t < i, 0.0, neg_inf)
            m_new = jnp.maximum(ms[g], cm + b)
            alpha = jnp.exp2(ms[g] - m_new)
            p = jnp.exp2(s_ref[slot, g] - (m_new - b))
            new_ms.append(m_new)
            new_as.append(alpha)
            p_ref[g] = p.astype(MXU_DTYPE)
        return tuple(new_as), tuple(new_ms), tuple(new_cms), t

    for g in heads:
        acc_ref[g] = jnp.ones(acc_ref.shape[1:], jnp.float32)
        p_ref[g] = jnp.zeros(p_ref.shape[1:], MXU_DTYPE)

    def tile(il, carry):
        a_pend, j_pend = carry
        finish(jnp.maximum(il - 1, 0), a_pend, j_pend)
        cur = jnp.minimum(il, tiles - 1)
        i = base + cur
        ms, cms = start(cur, i)

        def pair(t0, c):
            return stage(cur, i, t0 + 1, 1, stage(cur, i, t0, 0, c))

        pairs = jnp.where(il < tiles, (i + 1) // 2, 0)
        c = lax.fori_loop(0, pairs // 2, lambda u, c: pair(4 * u + 2, pair(4 * u, c)), (ones, ms, cms, i))
        a_pend, _, _, j_pend = lax.fori_loop(0, pairs % 2, lambda u, c: pair(4 * (pairs // 2), c), c)
        return a_pend, j_pend

    lax.fori_loop(0, tiles + 1, tile, (ones, jnp.int32(0)))


def _attention(qT, k, vT, kmean, tiles):
    B, H, nb, d, T = qT.shape
    S = nb * T
    dv = HEAD_DIM
    G = ATTN_HEADS
    gated = kmean is not None
    in_specs = [pl.BlockSpec((1, G, tiles, d, T), lambda b, h, i: (b, h, i, 0, 0)),
                pl.BlockSpec((1, G, S, d), lambda b, h, i: (b, h, 0, 0)),
                pl.BlockSpec((1, G, nb, VT_ROWS, T), lambda b, h, i: (b, h, 0, 0, 0))]
    args = [qT, k, vT]
    scratch = [pltpu.VMEM((G, VT_ROWS, T), jnp.float32),
               pltpu.VMEM((2, G, T, T), jnp.float32),
               pltpu.VMEM((G, T, T), MXU_DTYPE)]
    if gated:
        in_specs.append(pl.BlockSpec((1, G, nb, d), lambda b, h, i: (b, h, 0, 0)))
        args.append(kmean)
        scratch.append(pltpu.VMEM((G, nb, T), jnp.float32))
    return pl.pallas_call(
        functools.partial(_attn_kernel, gated=gated),
        grid=(B, H // G, nb // tiles),
        in_specs=in_specs,
        out_specs=pl.BlockSpec((1, tiles * T, G * dv), lambda b, h, i: (b, i, h)),
        out_shape=jax.ShapeDtypeStruct((B, S, H * dv), MXU_DTYPE),
        scratch_shapes=scratch,
        compiler_params=_params("parallel", "parallel", "parallel"),
        name="moba_attention" if gated else "mla_attention",
    )(*args)


def _layer_norm(y, g, b):
    mu = jnp.mean(y, axis=-1, keepdims=True)
    yc = y - mu
    var = jnp.mean(jnp.square(yc), axis=-1, keepdims=True)
    return yc * lax.rsqrt(var + LN_EPS) * g + b


def _outproj_kernel(a1_ref, a2_ref, w_ref, x_ref, g_ref, b_ref, o_ref):
    step = OUT_CHUNK
    k1 = a1_ref.shape[1]
    for r0 in range(0, x_ref.shape[0], step):
        r = slice(r0, r0 + step)
        mix = _dot(a1_ref[r, :], w_ref[0:k1, :]) + _dot(a2_ref[r, :], w_ref[k1:, :])
        o_ref[r, :] = _layer_norm(DEEPNORM_ALPHA * x_ref[r, :] + mix, g_ref[...], b_ref[...])


def _outproj_ln(a1, a2, w, x, g, b):
    M, D = x.shape
    rows = min(OUT_ROWS, M)
    K = a1.shape[1]
    full = lambda a: pl.BlockSpec(a.shape, lambda i: (0,) * a.ndim)
    return pl.pallas_call(
        _outproj_kernel,
        grid=(M // rows,),
        in_specs=[pl.BlockSpec((rows, K), lambda i: (i, 0)), pl.BlockSpec((rows, K), lambda i: (i, 0)),
                  pl.BlockSpec(w.shape, lambda i: (0, 0), pipeline_mode=pl.Buffered(1)),
                  pl.BlockSpec((rows, D), lambda i: (i, 0)), full(g), full(b)],
        out_specs=pl.BlockSpec((rows, D), lambda i: (i, 0)),
        out_shape=jax.ShapeDtypeStruct((M, D), jnp.float32),
        compiler_params=_params("parallel"),
        name="outproj_ln1",
    )(a1, a2, w, x, g, b)


def _ffn_kernel(x_ref, wup_ref, wdn_ref, g_ref, b_ref, o_ref, xb_ref, acc_ref):
    f = pl.program_id(1)
    last = pl.num_programs(1) - 1

    def partial(r):
        u = jnp.maximum(_dot(xb_ref[r, :], wup_ref[...]), 0.0)
        return _dot(jnp.square(u).astype(MXU_DTYPE), wdn_ref[...])

    @pl.when(f == 0)
    def _():
        xb_ref[...] = x_ref[...].astype(MXU_DTYPE)
        acc_ref[...] = partial(slice(None))

    @pl.when((f > 0) & (f < last))
    def _():
        acc_ref[...] += partial(slice(None))

    @pl.when(f == last)
    def _():
        for r0 in range(0, x_ref.shape[0], FFN_CHUNK):
            r = slice(r0, r0 + FFN_CHUNK)
            y = acc_ref[r, :] + partial(r)
            o_ref[r, :] = _layer_norm(DEEPNORM_ALPHA * x_ref[r, :] + y, g_ref[...], b_ref[...])


def _ffn_ln(x, wup, wdn, g, b):
    M, D = x.shape
    F = wup.shape[1]
    rows = min(FFN_ROWS, M)
    cols = min(FFN_COLS, F)
    vec = pl.BlockSpec((1, D), lambda i, f: (0, 0))
    return pl.pallas_call(
        _ffn_kernel,
        grid=(M // rows, F // cols),
        in_specs=[pl.BlockSpec((rows, D), lambda i, f: (i, 0)),
                  pl.BlockSpec((D, cols), lambda i, f: (0, f)),
                  pl.BlockSpec((cols, D), lambda i, f: (f, 0)),
                  vec, vec],
        out_specs=pl.BlockSpec((rows, D), lambda i, f: (i, 0)),
        out_shape=jax.ShapeDtypeStruct((M, D), jnp.float32),
        scratch_shapes=[pltpu.VMEM((rows, D), MXU_DTYPE), pltpu.VMEM((rows, D), jnp.float32)],
        compiler_params=_params("parallel", "arbitrary"),
        name="ffn_ln2",
    )(x, wup, wdn, g, b)


def _layer_weights(w_in, w_uq, w_ukv, w_out, w_up, w_down):
    D = w_in.shape[0]
    half = MLA_ROPE_DIM // 2
    swap = np.concatenate([np.arange(half, MLA_ROPE_DIM), np.arange(half)])
    c0 = MLA_Q_RANK + MLA_KV_RANK
    c1 = c0 + MLA_ROPE_DIM
    mw = MOBA_HEADS * HEAD_DIM
    cast = lambda w: w.astype(MXU_DTYPE)
    w_in, w_uq, w_ukv = cast(w_in), cast(w_uq), cast(w_ukv)
    w_lat = jnp.concatenate([w_in[:, :c1], w_in[:, c0:c1][:, swap]], axis=1)
    k_cols = w_in[:, c1 + mw:c1 + 2 * mw].reshape(D, MOBA_HEADS, HEAD_DIM)[:, :, MOBA_HEAD_PERM].reshape(D, mw)
    w_m = jnp.concatenate([w_in[:, c1:c1 + mw], k_cols, w_in[:, c1 + 2 * mw:c1 + 3 * mw]], axis=1)
    uq = w_uq.reshape(MLA_Q_RANK, MLA_HEADS, HEAD_DIM + MLA_ROPE_DIM)
    rope_cols = uq[:, :, HEAD_DIM:]
    uq = jnp.concatenate([uq, rope_cols[:, :, swap]], axis=-1).reshape(MLA_Q_RANK, MLA_HEADS * 2 * HEAD_DIM)
    ukv = w_ukv.reshape(MLA_KV_RANK, MLA_HEADS, 2, HEAD_DIM).transpose(0, 2, 1, 3).reshape(MLA_KV_RANK, -1)
    return dict(w_lat=w_lat, w_m=w_m, uq=uq, ukv=ukv, wo=cast(w_out), wup=cast(w_up), wdn=cast(w_down))


def kernel(x, positions, w_in, mla_q_norm, mla_kv_norm, w_uq, w_ukv, w_out, ln1_g, ln1_b, w_up, w_down, ln2_g, ln2_b):
    B, S, D = x.shape
    assert S % ATTN_TILE == 0 and w_in.shape[0] == DEPTH
    for l in range(DEPTH):
        w = _layer_weights(w_in[l], w_uq[l], w_ukv[l], w_out[l], w_up[l], w_down[l])
        row = lambda v: v[l].reshape(1, -1)
        qT_a, k_a, vT_a = _mla_prep(x, positions, w["w_lat"], row(mla_q_norm), row(mla_kv_norm), w["uq"], w["ukv"])
        qT_b, k_b, vT_b, kmean = _moba_prep(x, positions, w["w_m"])
        nb = S // ATTN_TILE
        out_a = _attention(qT_a, k_a, vT_a, None, min(MLA_TILES, nb))
        out_b = _attention(qT_b, k_b, vT_b, kmean, min(MOBA_TILES, nb))
        x1 = _outproj_ln(out_a.reshape(B * S, -1), out_b.reshape(B * S, -1), w["wo"],
                         x.reshape(B * S, D), row(ln1_g), row(ln1_b))
        x = _ffn_ln(x1, w["wup"], w["wdn"], row(ln2_g), row(ln2_b)).reshape(B, S, D)
    return x
```

```python
import functools
import math

import numpy as np
import jax
import jax.numpy as jnp
from jax import lax
from jax.experimental import pallas as pl
from jax.experimental.pallas import tpu as pltpu

HEAD_DIM = 128
MLA_HEADS = 8
MOBA_HEADS = 8
MLA_Q_RANK = 384
MLA_KV_RANK = 256
MLA_ROPE_DIM = 64
MOBA_BLOCK = 256
MOBA_TOPK = 3
ROPE_THETA = 500000.0
PARTIAL_ROT_DIM = 32
LN_EPS = 1e-5
RMS_EPS = 1e-6
DEPTH = 1
DEEPNORM_ALPHA = (2 * DEPTH) ** 0.25
MLA_SCALE = 1.0 / math.sqrt(HEAD_DIM + MLA_ROPE_DIM)
MOBA_SCALE = 1.0 / math.sqrt(HEAD_DIM)
LOG2E = math.log2(math.e)

LANES = 128
MXU_DTYPE = jnp.bfloat16
VMEM_LIMIT_BYTES = 56 * 1024 * 1024

ATTN_TILE = MOBA_BLOCK
ATTN_HEADS = 4
MLA_TILES = 8
MOBA_TILES = 16
BF16_SUBLANES = 16
VT_ROWS = HEAD_DIM + BF16_SUBLANES
MLA_PREP_ROWS = 1024
MLA_CHAIN_ROWS = 512
MOBA_PREP_ROWS = 1024
OUT_ROWS = 512
OUT_CHUNK = 256
FFN_ROWS = 512
FFN_COLS = 1024
FFN_CHUNK = 256


def _dot(a, b):
    return jnp.dot(a, b, preferred_element_type=jnp.float32)


def _params(*sem):
    return pltpu.CompilerParams(dimension_semantics=sem, vmem_limit_bytes=VMEM_LIMIT_BYTES)


def _inv_freq(dim):
    return ROPE_THETA ** (-jnp.arange(dim // 2, dtype=jnp.float32) * (2.0 / dim))


def _rms(x, g):
    y = x * lax.rsqrt(jnp.mean(jnp.square(x), axis=-1, keepdims=True) + RMS_EPS)
    return y * g


def _store_vT(vT_ref, h, blk, v):
    vT_ref[0, h, blk, 0:HEAD_DIM, :] = v.T.astype(MXU_DTYPE)
    row = lax.broadcasted_iota(jnp.int32, (BF16_SUBLANES, ATTN_TILE), 0)
    vT_ref[0, h, blk, HEAD_DIM:VT_ROWS, :] = jnp.where(row == 0, 1.0, 0.0).astype(MXU_DTYPE)


def _mla_prep_kernel(x_ref, pos_ref, invf_ref, wlat_ref, gq_ref, gkv_ref, wuq_ref, wukv_ref,
                     qT_ref, k_ref, vT_ref):
    rows = x_ref.shape[1]
    T = ATTN_TILE
    C = min(MLA_CHAIN_ROWS, rows)
    dq = 2 * HEAD_DIM
    xb = x_ref[0].astype(MXU_DTYPE)
    lats = [_dot(xb[c0:c0 + C], wlat_ref[...]) for c0 in range(0, rows, C)]
    for ci, c0 in enumerate(range(0, rows, C)):
        r = slice(c0, c0 + C)
        lat = lats[ci]
        cq = lat[:, :MLA_Q_RANK]
        ckv = lat[:, MLA_Q_RANK:MLA_Q_RANK + MLA_KV_RANK]
        kr2 = lat[:, MLA_Q_RANK + MLA_KV_RANK:]
        qall = _dot(_rms(cq, gq_ref[...]).astype(MXU_DTYPE), wuq_ref[...])
        kvall = _dot(_rms(ckv, gkv_ref[...]).astype(MXU_DTYPE), wukv_ref[...])
        ang = pos_ref[0, r, :].astype(jnp.float32) * invf_ref[...]
        lane = lax.broadcasted_iota(jnp.int32, ang.shape, 1)
        cos, sin = jnp.cos(ang), jnp.sin(ang)
        tab = jnp.where(lane < MLA_ROPE_DIM, cos, jnp.where(lane < MLA_ROPE_DIM + MLA_ROPE_DIM // 2, -sin, sin))
        t = kr2 * tab
        k_rope = jnp.where(lane < MLA_ROPE_DIM, t + pltpu.roll(t, MLA_ROPE_DIM, 1), 0.0).astype(MXU_DTYPE)
        qall = qall * (MLA_SCALE * LOG2E)
        for h in range(MLA_HEADS):
            nope = qall[:, h * dq:h * dq + HEAD_DIM]
            t = qall[:, h * dq + HEAD_DIM:(h + 1) * dq] * tab
            rope = t + pltpu.roll(t, MLA_ROPE_DIM, 1)
            k_ref[0, h, r, 0:HEAD_DIM] = kvall[:, h * HEAD_DIM:(h + 1) * HEAD_DIM].astype(MXU_DTYPE)
            k_ref[0, h, r, HEAD_DIM:dq] = k_rope
            v = kvall[:, (MLA_HEADS + h) * HEAD_DIM:(MLA_HEADS + h + 1) * HEAD_DIM]
            for t0 in range(0, C, T):
                blk = (c0 + t0) // T
                qT_ref[0, h, blk, 0:HEAD_DIM, :] = nope[t0:t0 + T].T.astype(MXU_DTYPE)
                qT_ref[0, h, blk, HEAD_DIM:dq, :] = rope[t0:t0 + T].T.astype(MXU_DTYPE)
                _store_vT(vT_ref, h, blk, v[t0:t0 + T])


def _mla_prep(x, positions, w_lat, gq, gkv, wuq, wukv):
    B, S, D = x.shape
    rows = min(MLA_PREP_ROWS, S)
    nb = S // ATTN_TILE
    H, dq = MLA_HEADS, 2 * HEAD_DIM
    invf = jnp.tile(_inv_freq(MLA_ROPE_DIM), 2 * LANES // MLA_ROPE_DIM).reshape(1, LANES)
    full = lambda a: pl.BlockSpec(a.shape, lambda b, i: (0,) * a.ndim)
    return pl.pallas_call(
        _mla_prep_kernel,
        grid=(B, S // rows),
        in_specs=[pl.BlockSpec((1, rows, D), lambda b, i: (b, i, 0)),
                  pl.BlockSpec((1, rows, 1), lambda b, i: (b, i, 0)),
                  full(invf), full(w_lat), full(gq), full(gkv), full(wuq), full(wukv)],
        out_specs=[pl.BlockSpec((1, H, rows // ATTN_TILE, dq, ATTN_TILE), lambda b, i: (b, 0, i, 0, 0)),
                   pl.BlockSpec((1, H, rows, dq), lambda b, i: (b, 0, i, 0)),
                   pl.BlockSpec((1, H, rows // ATTN_TILE, VT_ROWS, ATTN_TILE), lambda b, i: (b, 0, i, 0, 0))],
        out_shape=[jax.ShapeDtypeStruct((B, H, nb, dq, ATTN_TILE), MXU_DTYPE),
                   jax.ShapeDtypeStruct((B, H, S, dq), MXU_DTYPE),
                   jax.ShapeDtypeStruct((B, H, nb, VT_ROWS, ATTN_TILE), MXU_DTYPE)],
        compiler_params=_params("parallel", "parallel"),
        name="mla_prep",
    )(x, positions.reshape(B, S, 1), invf, w_lat, gq, gkv, wuq, wukv)


ROT_HALF = PARTIAL_ROT_DIM // 2
ROT_X2 = LANES // 2
MOBA_HEAD_PERM = np.concatenate([np.arange(0, ROT_HALF), np.arange(PARTIAL_ROT_DIM, PARTIAL_ROT_DIM + ROT_X2 - ROT_HALF),
                                 np.arange(ROT_HALF, PARTIAL_ROT_DIM),
                                 np.arange(PARTIAL_ROT_DIM + ROT_X2 - ROT_HALF, HEAD_DIM)])


def _moba_prep_kernel(x_ref, w_ref, posc_ref, posr_ref, invr_ref, invc_ref, qT_ref, k_ref, vT_ref, kmean_ref):
    i = pl.program_id(1)
    rows = x_ref.shape[1]
    nblk = rows // ATTN_TILE
    T = ATTN_TILE
    xb = x_ref[0].astype(MXU_DTYPE)
    ang = posc_ref[0].astype(jnp.float32) * invr_ref[...]
    lane = lax.broadcasted_iota(jnp.int32, ang.shape, 1)
    k_cos = jnp.cos(ang)
    k_sin = jnp.where(lane < ROT_X2, -jnp.sin(ang), jnp.sin(ang))
    ang_t = invc_ref[...] * posr_ref[0].astype(jnp.float32)
    q_cos, q_sin = jnp.cos(ang_t), jnp.sin(ang_t)
    q_scale = MOBA_SCALE * LOG2E
    width = 2 * HEAD_DIM
    per_kind = MOBA_HEADS * HEAD_DIM // width
    for chunk in range(3 * per_kind):
        hm = _dot(xb, w_ref[:, chunk * width:(chunk + 1) * width])
        kind, pair = divmod(chunk, per_kind)
        for hh in range(2):
            h = 2 * pair + hh
            xh = hm[:, hh * HEAD_DIM:(hh + 1) * HEAD_DIM]
            if kind == 0:
                for blk in range(nblk):
                    t = xh[blk * T:(blk + 1) * T].T * q_scale
                    c, s = q_cos[:, blk * T:(blk + 1) * T], q_sin[:, blk * T:(blk + 1) * T]
                    x1, x2 = t[0:ROT_HALF], t[ROT_HALF:PARTIAL_ROT_DIM]
                    split = PARTIAL_ROT_DIM + ROT_X2 - ROT_HALF
                    qT_ref[0, h, blk, 0:ROT_HALF, :] = (x1 * c - x2 * s).astype(MXU_DTYPE)
                    qT_ref[0, h, blk, ROT_HALF:ROT_X2, :] = t[PARTIAL_ROT_DIM:split].astype(MXU_DTYPE)
                    qT_ref[0, h, blk, ROT_X2:ROT_X2 + ROT_HALF, :] = (x2 * c + x1 * s).astype(MXU_DTYPE)
                    qT_ref[0, h, blk, ROT_X2 + ROT_HALF:HEAD_DIM, :] = t[split:].astype(MXU_DTYPE)
            elif kind == 1:
                k = xh * k_cos + pltpu.roll(xh, ROT_X2, 1) * k_sin
                k_ref[0, h] = k.astype(MXU_DTYPE)
                for blk in range(nblk):
                    mean = jnp.mean(k[blk * MOBA_BLOCK:(blk + 1) * MOBA_BLOCK], axis=0, keepdims=True)
                    kmean_ref[0, h, pl.ds(i * nblk + blk, 1), :] = mean
            else:
                for blk in range(nblk):
                    _store_vT(vT_ref, h, blk, xh[blk * T:(blk + 1) * T])


def _moba_prep(x, positions, w_m):
    B, S, D = x.shape
    rows = min(MOBA_PREP_ROWS, S)
    nb = S // ATTN_TILE
    H = MOBA_HEADS
    invf = _inv_freq(PARTIAL_ROT_DIM)
    gap = jnp.zeros((ROT_X2 - ROT_HALF,), jnp.float32)
    invr = jnp.concatenate([invf, gap, invf, gap]).reshape(1, LANES)
    invc = invf.reshape(ROT_HALF, 1)
    full = lambda a: pl.BlockSpec(a.shape, lambda b, i: (0,) * a.ndim)
    tile_t = lambda d: pl.BlockSpec((1, H, rows // ATTN_TILE, d, ATTN_TILE), lambda b, i: (b, 0, i, 0, 0))
    shape_t = lambda d: jax.ShapeDtypeStruct((B, H, nb, d, ATTN_TILE), MXU_DTYPE)
    return pl.pallas_call(
        _moba_prep_kernel,
        grid=(B, S // rows),
        in_specs=[pl.BlockSpec((1, rows, D), lambda b, i: (b, i, 0)),
                  pl.BlockSpec(w_m.shape, lambda b, i: (0, 0), pipeline_mode=pl.Buffered(1)),
                  pl.BlockSpec((1, rows, 1), lambda b, i: (b, i, 0)),
                  pl.BlockSpec((1, 1, rows), lambda b, i: (b, 0, i)),
                  full(invr), full(invc)],
        out_specs=[tile_t(HEAD_DIM),
                   pl.BlockSpec((1, H, rows, HEAD_DIM), lambda b, i: (b, 0, i, 0)),
                   tile_t(VT_ROWS),
                   pl.BlockSpec((1, H, nb, HEAD_DIM), lambda b, i: (b, 0, 0, 0))],
        out_shape=[shape_t(HEAD_DIM),
                   jax.ShapeDtypeStruct((B, H, S, HEAD_DIM), MXU_DTYPE),
                   shape_t(VT_ROWS),
                   jax.ShapeDtypeStruct((B, H, nb, HEAD_DIM), jnp.float32)],
        compiler_params=_params("parallel", "arbitrary"),
        name="moba_prep",
    )(x, w_m, positions.reshape(B, S, 1), positions.reshape(B, 1, S), invr, invc)


def _attn_kernel(*refs, gated):
    if gated:
        qT_ref, k_ref, vT_ref, kmean_ref, o_ref, acc_ref, s_ref, p_ref, bias_ref = refs
    else:
        qT_ref, k_ref, vT_ref, o_ref, acc_ref, s_ref, p_ref = refs
    T = ATTN_TILE
    G, tiles = qT_ref.shape[1], qT_ref.shape[2]
    nb, dv = vT_ref.shape[2], HEAD_DIM
    base = pl.program_id(2) * tiles
    neg_inf = -jnp.inf
    heads = range(G)
    ones = tuple(jnp.ones((1, T), jnp.float32) for _ in heads)

    def scores(g, il, j):
        kb = k_ref[0, g, pl.ds(pl.multiple_of(j * T, T), T), :]
        return _dot(kb, qT_ref[0, g, il])

    def pv(g, j, p):
        return _dot(vT_ref[0, g, j], p)

    def finish(il, a_pend, j_pend):
        accs = [a_pend[g] * acc_ref[g] + pv(g, j_pend, p_ref[g]) for g in heads]
        rows = pl.ds(pl.multiple_of(il * T, T), T)
        for g in heads:
            inv = 1.0 / accs[g][dv:dv + 1]
            o_ref[0, rows, g * dv:(g + 1) * dv] = (accs[g][0:dv] * inv).T.astype(o_ref.dtype)

    def start(il, i):
        if gated:
            gates = [_dot(kmean_ref[0, g].astype(MXU_DTYPE), qT_ref[0, g, il]) for g in heads]
        s_own = [scores(g, il, i) for g in heads]
        cms = []
        for g in heads:
            s_first = scores(g, il, 0)
            s_ref[0, g] = s_first
            cms.append(jnp.max(s_first, axis=0, keepdims=True))
        if gated:
            for g in heads:
                row = lax.broadcasted_iota(jnp.int32, gates[g].shape, 0)
                past = row < i
                gate = jnp.where(past, gates[g], neg_inf)
                taken = row < 0
                for _ in range(min(MOBA_TOPK, nb)):
                    cand = jnp.where(taken, neg_inf, gate)
                    best = jnp.max(cand, axis=0, keepdims=True)
                    hit = (cand == best) & jnp.logical_not(taken)
                    first = jnp.min(jnp.where(hit, row, nb), axis=0, keepdims=True)
                    taken = taken | (row == first)
                bias_ref[g] = jnp.where(taken & past, 0.0, neg_inf)
        kpos = lax.broadcasted_iota(jnp.int32, (T, T), 0)
        qpos = lax.broadcasted_iota(jnp.int32, (T, T), 1)
        causal = kpos <= qpos
        ms = []
        for g in heads:
            s = jnp.where(causal, s_own[g], neg_inf)
            m = jnp.max(s, axis=0, keepdims=True)
            ms.append(m)
            p_ref[g] = jnp.exp2(s - m).astype(MXU_DTYPE)
            acc_ref[g] = jnp.zeros(acc_ref.shape[1:], jnp.float32)
        return tuple(ms), tuple(cms)

    def stage(il, t, slot, carry, produce_next=True):
        a_pend, ms, cms, j_pend = carry
        for g in heads:
            acc_ref[g] = a_pend[g] * acc_ref[g] + pv(g, j_pend, p_ref[g])
        new_cms = []
        if produce_next:
            nxt = jnp.minimum(t + 1, nb - 1)
            for g in heads:
                s_next = scores(g, il, nxt)
                s_ref[1 - slot, g] = s_next
                new_cms.append(jnp.max(s_next, axis=0, keepdims=True))
        else:
            new_cms = cms
        new_ms, new_as = [], []
        for g in heads:
            if gated:
                b = bias_ref[g, pl.ds(t, 1), :]
                m_new = jnp.maximum(ms[g], cms[g] + b)
                p = jnp.exp2(s_ref[slot, g] - (m_new - b))
            else:
                m_new = jnp.maximum(ms[g], cms[g])
                p = jnp.exp2(s_ref[slot, g] - m_new)
            new_as.append(jnp.exp2(ms[g] - m_new))
            new_ms.append(m_new)
            p_ref[g] = p.astype(MXU_DTYPE)
        return tuple(new_as), tuple(new_ms), tuple(new_cms), t

    for g in heads:
        acc_ref[g] = jnp.ones(acc_ref.shape[1:], jnp.float32)
        p_ref[g] = jnp.zeros(p_ref.shape[1:], MXU_DTYPE)

    def tile(il, carry):
        a_pend, j_pend = carry
        finish(jnp.maximum(il - 1, 0), a_pend, j_pend)
        cur = jnp.minimum(il, tiles - 1)
        i = base + cur
        ms, cms = start(cur, i)

        def pair(t0, c):
            return stage(cur, t0 + 1, 1, stage(cur, t0, 0, c))

        n = jnp.where(il < tiles, i, 0)
        pairs = n // 2
        c = lax.fori_loop(0, pairs // 2, lambda u, c: pair(4 * u + 2, pair(4 * u, c)), (ones, ms, cms, i))
        c = lax.fori_loop(0, pairs % 2, lambda u, c: pair(4 * (pairs // 2), c), c)
        a_pend, _, _, j_pend = lax.fori_loop(
            0, n % 2, lambda u, c: stage(cur, n - 1, 0, c, produce_next=False), c)
        return a_pend, j_pend

    lax.fori_loop(0, tiles + 1, tile, (ones, jnp.int32(0)))


def _attention(qT, k, vT, kmean, tiles):
    B, H, nb, d, T = qT.shape
    S = nb * T
    dv = HEAD_DIM
    G = ATTN_HEADS
    gated = kmean is not None
    in_specs = [pl.BlockSpec((1, G, tiles, d, T), lambda b, h, i: (b, h, i, 0, 0)),
                pl.BlockSpec((1, G, S, d), lambda b, h, i: (b, h, 0, 0)),
                pl.BlockSpec((1, G, nb, VT_ROWS, T), lambda b, h, i: (b, h, 0, 0, 0))]
    args = [qT, k, vT]
    scratch = [pltpu.VMEM((G, VT_ROWS, T), jnp.float32),
               pltpu.VMEM((2, G, T, T), jnp.float32),
               pltpu.VMEM((G, T, T), MXU_DTYPE)]
    if gated:
        in_specs.append(pl.BlockSpec((1, G, nb, d), lambda b, h, i: (b, h, 0, 0)))
        args.append(kmean)
        scratch.append(pltpu.VMEM((G, nb, T), jnp.float32))
    return pl.pallas_call(
        functools.partial(_attn_kernel, gated=gated),
        grid=(B, H // G, nb // tiles),
        in_specs=in_specs,
        out_specs=pl.BlockSpec((1, tiles * T, G * dv), lambda b, h, i: (b, i, h)),
        out_shape=jax.ShapeDtypeStruct((B, S, H * dv), MXU_DTYPE),
        scratch_shapes=scratch,
        compiler_params=_params("parallel", "parallel", "parallel"),
        name="moba_attention" if gated else "mla_attention",
    )(*args)


def _layer_norm(y, g, b):
    mu = jnp.mean(y, axis=-1, keepdims=True)
    yc = y - mu
    var = jnp.mean(jnp.square(yc), axis=-1, keepdims=True)
    return yc * lax.rsqrt(var + LN_EPS) * g + b


def _outproj_kernel(a1_ref, a2_ref, w_ref, x_ref, g_ref, b_ref, o_ref):
    step = OUT_CHUNK
    k1 = a1_ref.shape[1]
    for r0 in range(0, x_ref.shape[0], step):
        r = slice(r0, r0 + step)
        mix = _dot(a1_ref[r, :], w_ref[0:k1, :]) + _dot(a2_ref[r, :], w_ref[k1:, :])
        o_ref[r, :] = _layer_norm(DEEPNORM_ALPHA * x_ref[r, :] + mix, g_ref[...], b_ref[...])


def _outproj_ln(a1, a2, w, x, g, b):
    M, D = x.shape
    rows = min(OUT_ROWS, M)
    K = a1.shape[1]
    full = lambda a: pl.BlockSpec(a.shape, lambda i: (0,) * a.ndim)
    return pl.pallas_call(
        _outproj_kernel,
        grid=(M // rows,),
        in_specs=[pl.BlockSpec((rows, K), lambda i: (i, 0)), pl.BlockSpec((rows, K), lambda i: (i, 0)),
                  pl.BlockSpec(w.shape, lambda i: (0, 0), pipeline_mode=pl.Buffered(1)),
                  pl.BlockSpec((rows, D), lambda i: (i, 0)), full(g), full(b)],
        out_specs=pl.BlockSpec((rows, D), lambda i: (i, 0)),
        out_shape=jax.ShapeDtypeStruct((M, D), jnp.float32),
        compiler_params=_params("parallel"),
        name="outproj_ln1",
    )(a1, a2, w, x, g, b)


def _ffn_kernel(x_ref, wup_ref, wdn_ref, g_ref, b_ref, o_ref, xb_ref, acc_ref):
    f = pl.program_id(1)
    last = pl.num_programs(1) - 1

    def partial(r):
        u = jnp.maximum(_dot(xb_ref[r, :], wup_ref[...]), 0.0)
        return _dot(jnp.square(u).astype(MXU_DTYPE), wdn_ref[...])

    @pl.when(f == 0)
    def _():
        xb_ref[...] = x_ref[...].astype(MXU_DTYPE)
        acc_ref[...] = partial(slice(None))

    @pl.when((f > 0) & (f < last))
    def _():
        acc_ref[...] += partial(slice(None))

    @pl.when(f == last)
    def _():
        for r0 in range(0, x_ref.shape[0], FFN_CHUNK):
            r = slice(r0, r0 + FFN_CHUNK)
            y = acc_ref[r, :] + partial(r)
            o_ref[r, :] = _layer_norm(DEEPNORM_ALPHA * x_ref[r, :] + y, g_ref[...], b_ref[...])


def _ffn_ln(x, wup, wdn, g, b):
    M, D = x.shape
    F = wup.shape[1]
    rows = min(FFN_ROWS, M)
    cols = min(FFN_COLS, F)
    vec = pl.BlockSpec((1, D), lambda i, f: (0, 0))
    return pl.pallas_call(
        _ffn_kernel,
        grid=(M // rows, F // cols),
        in_specs=[pl.BlockSpec((rows, D), lambda i, f: (i, 0)),
                  pl.BlockSpec((D, cols), lambda i, f: (0, f)),
                  pl.BlockSpec((cols, D), lambda i, f: (f, 0)),
                  vec, vec],
        out_specs=pl.BlockSpec((rows, D), lambda i, f: (i, 0)),
        out_shape=jax.ShapeDtypeStruct((M, D), jnp.float32),
        scratch_shapes=[pltpu.VMEM((rows, D), MXU_DTYPE), pltpu.VMEM((rows, D), jnp.float32)],
        compiler_params=_params("parallel", "arbitrary"),
        name="ffn_ln2",
    )(x, wup, wdn, g, b)


def _layer_weights(w_in, w_uq, w_ukv, w_out, w_up, w_down):
    D = w_in.shape[0]
    half = MLA_ROPE_DIM // 2
    swap = np.concatenate([np.arange(half, MLA_ROPE_DIM), np.arange(half)])
    c0 = MLA_Q_RANK + MLA_KV_RANK
    c1 = c0 + MLA_ROPE_DIM
    mw = MOBA_HEADS * HEAD_DIM
    cast = lambda w: w.astype(MXU_DTYPE)
    w_in, w_uq, w_ukv = cast(w_in), cast(w_uq), cast(w_ukv)
    w_lat = jnp.concatenate([w_in[:, :c1], w_in[:, c0:c1][:, swap]], axis=1)
    k_cols = w_in[:, c1 + mw:c1 + 2 * mw].reshape(D, MOBA_HEADS, HEAD_DIM)[:, :, MOBA_HEAD_PERM].reshape(D, mw)
    w_m = jnp.concatenate([w_in[:, c1:c1 + mw], k_cols, w_in[:, c1 + 2 * mw:c1 + 3 * mw]], axis=1)
    uq = w_uq.reshape(MLA_Q_RANK, MLA_HEADS, HEAD_DIM + MLA_ROPE_DIM)
    rope_cols = uq[:, :, HEAD_DIM:]
    uq = jnp.concatenate([uq, rope_cols[:, :, swap]], axis=-1).reshape(MLA_Q_RANK, MLA_HEADS * 2 * HEAD_DIM)
    ukv = w_ukv.reshape(MLA_KV_RANK, MLA_HEADS, 2, HEAD_DIM).transpose(0, 2, 1, 3).reshape(MLA_KV_RANK, -1)
    return dict(w_lat=w_lat, w_m=w_m, uq=uq, ukv=ukv, wo=cast(w_out), wup=cast(w_up), wdn=cast(w_down))


def kernel(x, positions, w_in, mla_q_norm, mla_kv_norm, w_uq, w_ukv, w_out, ln1_g, ln1_b, w_up, w_down, ln2_g, ln2_b):
    B, S, D = x.shape
    assert S % ATTN_TILE == 0 and w_in.shape[0] == DEPTH
    for l in range(DEPTH):
        w = _layer_weights(w_in[l], w_uq[l], w_ukv[l], w_out[l], w_up[l], w_down[l])
        row = lambda v: v[l].reshape(1, -1)
        qT_a, k_a, vT_a = _mla_prep(x, positions, w["w_lat"], row(mla_q_norm), row(mla_kv_norm), w["uq"], w["ukv"])
        qT_b, k_b, vT_b, kmean = _moba_prep(x, positions, w["w_m"])
        nb = S // ATTN_TILE
        out_a = _attention(qT_a, k_a, vT_a, None, min(MLA_TILES, nb))
        out_b = _attention(qT_b, k_b, vT_b, kmean, min(MOBA_TILES, nb))
        x1 = _outproj_ln(out_a.reshape(B * S, -1), out_b.reshape(B * S, -1), w["wo"],
                         x.reshape(B * S, D), row(ln1_g), row(ln1_b))
        x = _ffn_ln(x1, w["wup"], w["wdn"], row(ln2_g), row(ln2_b)).reshape(B, S, D)
    return x
```

```python
import functools
import math

import numpy as np
import jax
import jax.numpy as jnp
from jax import lax
from jax.experimental import pallas as pl
from jax.experimental.pallas import tpu as pltpu

HEAD_DIM = 128
MLA_HEADS = 8
MOBA_HEADS = 8
MLA_Q_RANK = 384
MLA_KV_RANK = 256
MLA_ROPE_DIM = 64
MOBA_BLOCK = 256
MOBA_TOPK = 3
ROPE_THETA = 500000.0
PARTIAL_ROT_DIM = 32
LN_EPS = 1e-5
RMS_EPS = 1e-6
DEPTH = 1
DEEPNORM_ALPHA = (2 * DEPTH) ** 0.25
MLA_SCALE = 1.0 / math.sqrt(HEAD_DIM + MLA_ROPE_DIM)
MOBA_SCALE = 1.0 / math.sqrt(HEAD_DIM)
LOG2E = math.log2(math.e)

LANES = 128
MXU_DTYPE = jnp.bfloat16
VMEM_LIMIT_BYTES = 56 * 1024 * 1024

ATTN_TILE = MOBA_BLOCK
ATTN_HEADS = 4
MLA_TILES = 8
MOBA_TILES = 16
BF16_SUBLANES = 16
VT_ROWS = HEAD_DIM + BF16_SUBLANES
MLA_PREP_ROWS = 1024
MLA_CHAIN_ROWS = 512
MOBA_PREP_ROWS = 1024
OUT_ROWS = 512
OUT_CHUNK = 256
FFN_ROWS = 512
FFN_COLS = 1024
FFN_CHUNK = 256


def _dot(a, b):
    return jnp.dot(a, b, preferred_element_type=jnp.float32)


def _params(*sem):
    return pltpu.CompilerParams(dimension_semantics=sem, vmem_limit_bytes=VMEM_LIMIT_BYTES)


def _inv_freq(dim):
    return ROPE_THETA ** (-jnp.arange(dim // 2, dtype=jnp.float32) * (2.0 / dim))


def _rms(x, g):
    y = x * lax.rsqrt(jnp.mean(jnp.square(x), axis=-1, keepdims=True) + RMS_EPS)
    return y * g


def _store_vT(vT_ref, h, blk, v):
    vT_ref[0, h, blk, 0:HEAD_DIM, :] = v.T.astype(MXU_DTYPE)
    row = lax.broadcasted_iota(jnp.int32, (BF16_SUBLANES, ATTN_TILE), 0)
    vT_ref[0, h, blk, HEAD_DIM:VT_ROWS, :] = jnp.where(row == 0, 1.0, 0.0).astype(MXU_DTYPE)


def _mla_prep_kernel(x_ref, pos_ref, invf_ref, wlat_ref, gq_ref, gkv_ref, wuq_ref, wukv_ref,
                     qT_ref, k_ref, vT_ref):
    rows = x_ref.shape[1]
    T = ATTN_TILE
    C = min(MLA_CHAIN_ROWS, rows)
    dq = 2 * HEAD_DIM
    xb = x_ref[0].astype(MXU_DTYPE)
    lats = [_dot(xb[c0:c0 + C], wlat_ref[...]) for c0 in range(0, rows, C)]
    for ci, c0 in enumerate(range(0, rows, C)):
        r = slice(c0, c0 + C)
        lat = lats[ci]
        cq = lat[:, :MLA_Q_RANK]
        ckv = lat[:, MLA_Q_RANK:MLA_Q_RANK + MLA_KV_RANK]
        kr2 = lat[:, MLA_Q_RANK + MLA_KV_RANK:]
        qall = _dot(_rms(cq, gq_ref[...]).astype(MXU_DTYPE), wuq_ref[...])
        kvall = _dot(_rms(ckv, gkv_ref[...]).astype(MXU_DTYPE), wukv_ref[...])
        ang = pos_ref[0, r, :].astype(jnp.float32) * invf_ref[...]
        lane = lax.broadcasted_iota(jnp.int32, ang.shape, 1)
        cos, sin = jnp.cos(ang), jnp.sin(ang)
        tab = jnp.where(lane < MLA_ROPE_DIM, cos, jnp.where(lane < MLA_ROPE_DIM + MLA_ROPE_DIM // 2, -sin, sin))
        t = kr2 * tab
        k_rope = jnp.where(lane < MLA_ROPE_DIM, t + pltpu.roll(t, MLA_ROPE_DIM, 1), 0.0).astype(MXU_DTYPE)
        qall = qall * (MLA_SCALE * LOG2E)
        for h in range(MLA_HEADS):
            nope = qall[:, h * dq:h * dq + HEAD_DIM]
            t = qall[:, h * dq + HEAD_DIM:(h + 1) * dq] * tab
            rope = t + pltpu.roll(t, MLA_ROPE_DIM, 1)
            k_ref[0, h, r, 0:HEAD_DIM] = kvall[:, h * HEAD_DIM:(h + 1) * HEAD_DIM].astype(MXU_DTYPE)
            k_ref[0, h, r, HEAD_DIM:dq] = k_rope
            v = kvall[:, (MLA_HEADS + h) * HEAD_DIM:(MLA_HEADS + h + 1) * HEAD_DIM]
            for t0 in range(0, C, T):
                blk = (c0 + t0) // T
                qT_ref[0, h, blk, 0:HEAD_DIM, :] = nope[t0:t0 + T].T.astype(MXU_DTYPE)
                qT_ref[0, h, blk, HEAD_DIM:dq, :] = rope[t0:t0 + T].T.astype(MXU_DTYPE)
                _store_vT(vT_ref, h, blk, v[t0:t0 + T])


def _mla_prep(x, positions, w_lat, gq, gkv, wuq, wukv):
    B, S, D = x.shape
    rows = min(MLA_PREP_ROWS, S)
    nb = S // ATTN_TILE
    H, dq = MLA_HEADS, 2 * HEAD_DIM
    invf = jnp.tile(_inv_freq(MLA_ROPE_DIM), 2 * LANES // MLA_ROPE_DIM).reshape(1, LANES)
    full = lambda a: pl.BlockSpec(a.shape, lambda b, i: (0,) * a.ndim)
    return pl.pallas_call(
        _mla_prep_kernel,
        grid=(B, S // rows),
        in_specs=[pl.BlockSpec((1, rows, D), lambda b, i: (b, i, 0)),
                  pl.BlockSpec((1, rows, 1), lambda b, i: (b, i, 0)),
                  full(invf), full(w_lat), full(gq), full(gkv), full(wuq), full(wukv)],
        out_specs=[pl.BlockSpec((1, H, rows // ATTN_TILE, dq, ATTN_TILE), lambda b, i: (b, 0, i, 0, 0)),
                   pl.BlockSpec((1, H, rows, dq), lambda b, i: (b, 0, i, 0)),
                   pl.BlockSpec((1, H, rows // ATTN_TILE, VT_ROWS, ATTN_TILE), lambda b, i: (b, 0, i, 0, 0))],
        out_shape=[jax.ShapeDtypeStruct((B, H, nb, dq, ATTN_TILE), MXU_DTYPE),
                   jax.ShapeDtypeStruct((B, H, S, dq), MXU_DTYPE),
                   jax.ShapeDtypeStruct((B, H, nb, VT_ROWS, ATTN_TILE), MXU_DTYPE)],
        compiler_params=_params("parallel", "parallel"),
        name="mla_prep",
    )(x, positions.reshape(B, S, 1), invf, w_lat, gq, gkv, wuq, wukv)


ROT_HALF = PARTIAL_ROT_DIM // 2
ROT_X2 = LANES // 2
MOBA_HEAD_PERM = np.concatenate([np.arange(0, ROT_HALF), np.arange(PARTIAL_ROT_DIM, PARTIAL_ROT_DIM + ROT_X2 - ROT_HALF),
                                 np.arange(ROT_HALF, PARTIAL_ROT_DIM),
                                 np.arange(PARTIAL_ROT_DIM + ROT_X2 - ROT_HALF, HEAD_DIM)])


def _moba_prep_kernel(x_ref, w_ref, posc_ref, posr_ref, invr_ref, invc_ref, qT_ref, k_ref, vT_ref, kmean_ref):
    i = pl.program_id(1)
    rows = x_ref.shape[1]
    nblk = rows // ATTN_TILE
    T = ATTN_TILE
    xb = x_ref[0].astype(MXU_DTYPE)
    ang = posc_ref[0].astype(jnp.float32) * invr_ref[...]
    lane = lax.broadcasted_iota(jnp.int32, ang.shape, 1)
    k_cos = jnp.cos(ang)
    k_sin = jnp.where(lane < ROT_X2, -jnp.sin(ang), jnp.sin(ang))
    ang_t = invc_ref[...] * posr_ref[0].astype(jnp.float32)
    q_cos, q_sin = jnp.cos(ang_t), jnp.sin(ang_t)
    q_scale = MOBA_SCALE * LOG2E
    width = 2 * HEAD_DIM
    per_kind = MOBA_HEADS * HEAD_DIM // width
    for chunk in range(3 * per_kind):
        hm = _dot(xb, w_ref[:, chunk * width:(chunk + 1) * width])
        kind, pair = divmod(chunk, per_kind)
        for hh in range(2):
            h = 2 * pair + hh
            xh = hm[:, hh * HEAD_DIM:(hh + 1) * HEAD_DIM]
            if kind == 0:
                for blk in range(nblk):
                    t = xh[blk * T:(blk + 1) * T].T * q_scale
                    c, s = q_cos[:, blk * T:(blk + 1) * T], q_sin[:, blk * T:(blk + 1) * T]
                    x1, x2 = t[0:ROT_HALF], t[ROT_HALF:PARTIAL_ROT_DIM]
                    split = PARTIAL_ROT_DIM + ROT_X2 - ROT_HALF
                    qT_ref[0, h, blk, 0:ROT_HALF, :] = (x1 * c - x2 * s).astype(MXU_DTYPE)
                    qT_ref[0, h, blk, ROT_HALF:ROT_X2, :] = t[PARTIAL_ROT_DIM:split].astype(MXU_DTYPE)
                    qT_ref[0, h, blk, ROT_X2:ROT_X2 + ROT_HALF, :] = (x2 * c + x1 * s).astype(MXU_DTYPE)
                    qT_ref[0, h, blk, ROT_X2 + ROT_HALF:HEAD_DIM, :] = t[split:].astype(MXU_DTYPE)
            elif kind == 1:
                k = xh * k_cos + pltpu.roll(xh, ROT_X2, 1) * k_sin
                k_ref[0, h] = k.astype(MXU_DTYPE)
                for blk in range(nblk):
                    mean = jnp.mean(k[blk * MOBA_BLOCK:(blk + 1) * MOBA_BLOCK], axis=0, keepdims=True)
                    kmean_ref[0, h, pl.ds(i * nblk + blk, 1), :] = mean
            else:
                for blk in range(nblk):
                    _store_vT(vT_ref, h, blk, xh[blk * T:(blk + 1) * T])


def _moba_prep(x, positions, w_m):
    B, S, D = x.shape
    rows = min(MOBA_PREP_ROWS, S)
    nb = S // ATTN_TILE
    H = MOBA_HEADS
    invf = _inv_freq(PARTIAL_ROT_DIM)
    gap = jnp.zeros((ROT_X2 - ROT_HALF,), jnp.float32)
    invr = jnp.concatenate([invf, gap, invf, gap]).reshape(1, LANES)
    invc = invf.reshape(ROT_HALF, 1)
    full = lambda a: pl.BlockSpec(a.shape, lambda b, i: (0,) * a.ndim)
    tile_t = lambda d: pl.BlockSpec((1, H, rows // ATTN_TILE, d, ATTN_TILE), lambda b, i: (b, 0, i, 0, 0))
    shape_t = lambda d: jax.ShapeDtypeStruct((B, H, nb, d, ATTN_TILE), MXU_DTYPE)
    return pl.pallas_call(
        _moba_prep_kernel,
        grid=(B, S // rows),
        in_specs=[pl.BlockSpec((1, rows, D), lambda b, i: (b, i, 0)),
                  pl.BlockSpec(w_m.shape, lambda b, i: (0, 0), pipeline_mode=pl.Buffered(1)),
                  pl.BlockSpec((1, rows, 1), lambda b, i: (b, i, 0)),
                  pl.BlockSpec((1, 1, rows), lambda b, i: (b, 0, i)),
                  full(invr), full(invc)],
        out_specs=[tile_t(HEAD_DIM),
                   pl.BlockSpec((1, H, rows, HEAD_DIM), lambda b, i: (b, 0, i, 0)),
                   tile_t(VT_ROWS),
                   pl.BlockSpec((1, H, nb, HEAD_DIM), lambda b, i: (b, 0, 0, 0))],
        out_shape=[shape_t(HEAD_DIM),
                   jax.ShapeDtypeStruct((B, H, S, HEAD_DIM), MXU_DTYPE),
                   shape_t(VT_ROWS),
                   jax.ShapeDtypeStruct((B, H, nb, HEAD_DIM), jnp.float32)],
        compiler_params=_params("parallel", "arbitrary"),
        name="moba_prep",
    )(x, w_m, positions.reshape(B, S, 1), positions.reshape(B, 1, S), invr, invc)


def _attn_kernel(*refs, gated):
    if gated:
        qT_ref, k_ref, vT_ref, kmean_ref, o_ref, acc_ref, s_ref, p_ref, bias_ref = refs
    else:
        qT_ref, k_ref, vT_ref, o_ref, acc_ref, s_ref, p_ref = refs
    T = ATTN_TILE
    G, tiles = qT_ref.shape[1], qT_ref.shape[2]
    nb, dv = vT_ref.shape[2], HEAD_DIM
    base = pl.program_id(2) * tiles
    neg_inf = -jnp.inf
    heads = range(G)
    ones = tuple(jnp.ones((1, T), jnp.float32) for _ in heads)

    def scores(g, il, j):
        kb = k_ref[0, g, pl.ds(pl.multiple_of(j * T, T), T), :]
        return _dot(kb, qT_ref[0, g, il])

    def pv(g, j, p):
        return _dot(vT_ref[0, g, j], p)

    def finish(il, a_pend, j_pend):
        accs = [a_pend[g] * acc_ref[g] + pv(g, j_pend, p_ref[g]) for g in heads]
        rows = pl.ds(pl.multiple_of(il * T, T), T)
        for g in heads:
            inv = 1.0 / accs[g][dv:dv + 1]
            o_ref[0, rows, g * dv:(g + 1) * dv] = (accs[g][0:dv] * inv).T.astype(o_ref.dtype)

    def start(il, i):
        if gated:
            gates = [_dot(kmean_ref[0, g].astype(MXU_DTYPE), qT_ref[0, g, il]) for g in heads]
        s_own = [scores(g, il, i) for g in heads]
        cms = []
        for g in heads:
            s_first = scores(g, il, 0)
            s_ref[0, g] = s_first
            cms.append(jnp.max(s_first, axis=0, keepdims=True))
        if gated:
            for g in heads:
                row = lax.broadcasted_iota(jnp.int32, gates[g].shape, 0)
                past = row < i
                gate = jnp.where(past, gates[g], neg_inf)
                taken = row < 0
                for _ in range(min(MOBA_TOPK, nb)):
                    cand = jnp.where(taken, neg_inf, gate)
                    best = jnp.max(cand, axis=0, keepdims=True)
                    hit = (cand == best) & jnp.logical_not(taken)
                    first = jnp.min(jnp.where(hit, row, nb), axis=0, keepdims=True)
                    taken = taken | (row == first)
                bias_ref[g] = jnp.where(taken & past, 0.0, neg_inf)
        kpos = lax.broadcasted_iota(jnp.int32, (T, T), 0)
        qpos = lax.broadcasted_iota(jnp.int32, (T, T), 1)
        causal = kpos <= qpos
        ms = []
        for g in heads:
            s = jnp.where(causal, s_own[g], neg_inf)
            m = jnp.max(s, axis=0, keepdims=True)
            ms.append(m)
            p_ref[g] = jnp.exp2(s - m).astype(MXU_DTYPE)
            acc_ref[g] = jnp.zeros(acc_ref.shape[1:], jnp.float32)
        return tuple(ms), tuple(cms)

    def stage(il, t, slot, carry, produce_next=True):
        a_pend, ms, cms, j_pend = carry
        for g in heads:
            acc_ref[g] = a_pend[g] * acc_ref[g] + pv(g, j_pend, p_ref[g])
        new_cms = []
        if produce_next:
            nxt = jnp.minimum(t + 1, nb - 1)
            for g in heads:
                s_next = scores(g, il, nxt)
                s_ref[1 - slot, g] = s_next
                new_cms.append(jnp.max(s_next, axis=0, keepdims=True))
        else:
            new_cms = cms
        new_ms, new_as = [], []
        for g in heads:
            if gated:
                b = bias_ref[g, pl.ds(t, 1), :]
                m_new = jnp.maximum(ms[g], cms[g] + b)
                p = jnp.exp2(s_ref[slot, g] - (m_new - b))
            else:
                m_new = jnp.maximum(ms[g], cms[g])
                p = jnp.exp2(s_ref[slot, g] - m_new)
            new_as.append(jnp.exp2(ms[g] - m_new))
            new_ms.append(m_new)
            p_ref[g] = p.astype(MXU_DTYPE)
        return tuple(new_as), tuple(new_ms), tuple(new_cms), t

    for g in heads:
        acc_ref[g] = jnp.ones(acc_ref.shape[1:], jnp.float32)
        p_ref[g] = jnp.zeros(p_ref.shape[1:], MXU_DTYPE)

    def tile(il, carry):
        a_pend, j_pend = carry
        finish(jnp.maximum(il - 1, 0), a_pend, j_pend)
        cur = jnp.minimum(il, tiles - 1)
        i = base + cur
        ms, cms = start(cur, i)

        def pair(t0, c):
            return stage(cur, t0 + 1, 1, stage(cur, t0, 0, c))

        def quad(t0, c):
            return pair(t0 + 2, pair(t0, c))

        n = jnp.where(il < tiles, i, 0)
        pairs = n // 2
        c = lax.fori_loop(0, pairs // 4, lambda u, c: quad(8 * u + 4, quad(8 * u, c)), (ones, ms, cms, i))
        c = lax.fori_loop(0, (pairs % 4) // 2, lambda u, c: quad(8 * (pairs // 4), c), c)
        c = lax.fori_loop(0, pairs % 2, lambda u, c: pair(4 * (pairs // 2), c), c)
        a_pend, _, _, j_pend = lax.fori_loop(
            0, n % 2, lambda u, c: stage(cur, n - 1, 0, c, produce_next=False), c)
        return a_pend, j_pend

    lax.fori_loop(0, tiles + 1, tile, (ones, jnp.int32(0)))


def _attention(qT, k, vT, kmean, tiles):
    B, H, nb, d, T = qT.shape
    S = nb * T
    dv = HEAD_DIM
    G = ATTN_HEADS
    gated = kmean is not None
    in_specs = [pl.BlockSpec((1, G, tiles, d, T), lambda b, h, i: (b, h, i, 0, 0)),
                pl.BlockSpec((1, G, S, d), lambda b, h, i: (b, h, 0, 0)),
                pl.BlockSpec((1, G, nb, VT_ROWS, T), lambda b, h, i: (b, h, 0, 0, 0))]
    args = [qT, k, vT]
    scratch = [pltpu.VMEM((G, VT_ROWS, T), jnp.float32),
               pltpu.VMEM((2, G, T, T), jnp.float32),
               pltpu.VMEM((G, T, T), MXU_DTYPE)]
    if gated:
        in_specs.append(pl.BlockSpec((1, G, nb, d), lambda b, h, i: (b, h, 0, 0)))
        args.append(kmean)
        scratch.append(pltpu.VMEM((G, nb, T), jnp.float32))
    return pl.pallas_call(
        functools.partial(_attn_kernel, gated=gated),
        grid=(B, H // G, nb // tiles),
        in_specs=in_specs,
        out_specs=pl.BlockSpec((1, tiles * T, G * dv), lambda b, h, i: (b, i, h)),
        out_shape=jax.ShapeDtypeStruct((B, S, H * dv), MXU_DTYPE),
        scratch_shapes=scratch,
        compiler_params=_params("parallel", "parallel", "parallel"),
        name="moba_attention" if gated else "mla_attention",
    )(*args)


def _layer_norm(y, g, b):
    mu = jnp.mean(y, axis=-1, keepdims=True)
    yc = y - mu
    var = jnp.mean(jnp.square(yc), axis=-1, keepdims=True)
    return yc * lax.rsqrt(var + LN_EPS) * g + b


def _outproj_kernel(a1_ref, a2_ref, w_ref, x_ref, g_ref, b_ref, o_ref):
    step = OUT_CHUNK
    k1 = a1_ref.shape[1]
    for r0 in range(0, x_ref.shape[0], step):
        r = slice(r0, r0 + step)
        mix = _dot(a1_ref[r, :], w_ref[0:k1, :]) + _dot(a2_ref[r, :], w_ref[k1:, :])
        o_ref[r, :] = _layer_norm(DEEPNORM_ALPHA * x_ref[r, :] + mix, g_ref[...], b_ref[...])


def _outproj_ln(a1, a2, w, x, g, b):
    M, D = x.shape
    rows = min(OUT_ROWS, M)
    K = a1.shape[1]
    full = lambda a: pl.BlockSpec(a.shape, lambda i: (0,) * a.ndim)
    return pl.pallas_call(
        _outproj_kernel,
        grid=(M // rows,),
        in_specs=[pl.BlockSpec((rows, K), lambda i: (i, 0)), pl.BlockSpec((rows, K), lambda i: (i, 0)),
                  pl.BlockSpec(w.shape, lambda i: (0, 0), pipeline_mode=pl.Buffered(1)),
                  pl.BlockSpec((rows, D), lambda i: (i, 0)), full(g), full(b)],
        out_specs=pl.BlockSpec((rows, D), lambda i: (i, 0)),
        out_shape=jax.ShapeDtypeStruct((M, D), jnp.float32),
        compiler_params=_params("parallel"),
        name="outproj_ln1",
    )(a1, a2, w, x, g, b)


def _ffn_kernel(x_ref, wup_ref, wdn_ref, g_ref, b_ref, o_ref, xb_ref, acc_ref):
    f = pl.program_id(1)
    last = pl.num_programs(1) - 1

    def partial(r):
        u = jnp.maximum(_dot(xb_ref[r, :], wup_ref[...]), 0.0)
        return _dot(jnp.square(u).astype(MXU_DTYPE), wdn_ref[...])

    @pl.when(f == 0)
    def _():
        xb_ref[...] = x_ref[...].astype(MXU_DTYPE)
        acc_ref[...] = partial(slice(None))

    @pl.when((f > 0) & (f < last))
    def _():
        acc_ref[...] += partial(slice(None))

    @pl.when(f == last)
    def _():
        for r0 in range(0, x_ref.shape[0], FFN_CHUNK):
            r = slice(r0, r0 + FFN_CHUNK)
            y = acc_ref[r, :] + partial(r)
            o_ref[r, :] = _layer_norm(DEEPNORM_ALPHA * x_ref[r, :] + y, g_ref[...], b_ref[...])


def _ffn_ln(x, wup, wdn, g, b):
    M, D = x.shape
    F = wup.shape[1]
    rows = min(FFN_ROWS, M)
    cols = min(FFN_COLS, F)
    vec = pl.BlockSpec((1, D), lambda i, f: (0, 0))
    return pl.pallas_call(
        _ffn_kernel,
        grid=(M // rows, F // cols),
        in_specs=[pl.BlockSpec((rows, D), lambda i, f: (i, 0)),
                  pl.BlockSpec((D, cols), lambda i, f: (0, f)),
                  pl.BlockSpec((cols, D), lambda i, f: (f, 0)),
                  vec, vec],
        out_specs=pl.BlockSpec((rows, D), lambda i, f: (i, 0)),
        out_shape=jax.ShapeDtypeStruct((M, D), jnp.float32),
        scratch_shapes=[pltpu.VMEM((rows, D), MXU_DTYPE), pltpu.VMEM((rows, D), jnp.float32)],
        compiler_params=_params("parallel", "arbitrary"),
        name="ffn_ln2",
    )(x, wup, wdn, g, b)


def _layer_weights(w_in, w_uq, w_ukv, w_out, w_up, w_down):
    D = w_in.shape[0]
    half = MLA_ROPE_DIM // 2
    swap = np.concatenate([np.arange(half, MLA_ROPE_DIM), np.arange(half)])
    c0 = MLA_Q_RANK + MLA_KV_RANK
    c1 = c0 + MLA_ROPE_DIM
    mw = MOBA_HEADS * HEAD_DIM
    cast = lambda w: w.astype(MXU_DTYPE)
    w_in, w_uq, w_ukv = cast(w_in), cast(w_uq), cast(w_ukv)
    w_lat = jnp.concatenate([w_in[:, :c1], w_in[:, c0:c1][:, swap]], axis=1)
    runs = np.split(MOBA_HEAD_PERM, np.flatnonzero(np.diff(MOBA_HEAD_PERM) != 1) + 1)
    k0 = c1 + mw
    k_cols = [w_in[:, k0 + h * HEAD_DIM + r[0]:k0 + h * HEAD_DIM + r[-1] + 1] for h in range(MOBA_HEADS) for r in runs]
    w_m = jnp.concatenate([w_in[:, c1:c1 + mw], *k_cols, w_in[:, c1 + 2 * mw:c1 + 3 * mw]], axis=1)
    uq = w_uq.reshape(MLA_Q_RANK, MLA_HEADS, HEAD_DIM + MLA_ROPE_DIM)
    rope_cols = uq[:, :, HEAD_DIM:]
    uq = jnp.concatenate([uq, rope_cols[:, :, swap]], axis=-1).reshape(MLA_Q_RANK, MLA_HEADS * 2 * HEAD_DIM)
    ukv = w_ukv.reshape(MLA_KV_RANK, MLA_HEADS, 2, HEAD_DIM).transpose(0, 2, 1, 3).reshape(MLA_KV_RANK, -1)
    return dict(w_lat=w_lat, w_m=w_m, uq=uq, ukv=ukv, wo=cast(w_out), wup=cast(w_up), wdn=cast(w_down))


def kernel(x, positions, w_in, mla_q_norm, mla_kv_norm, w_uq, w_ukv, w_out, ln1_g, ln1_b, w_up, w_down, ln2_g, ln2_b):
    B, S, D = x.shape
    assert S % ATTN_TILE == 0 and w_in.shape[0] == DEPTH
    for l in range(DEPTH):
        w = _layer_weights(w_in[l], w_uq[l], w_ukv[l], w_out[l], w_up[l], w_down[l])
        row = lambda v: v[l].reshape(1, -1)
        qT_a, k_a, vT_a = _mla_prep(x, positions, w["w_lat"], row(mla_q_norm), row(mla_kv_norm), w["uq"], w["ukv"])
        qT_b, k_b, vT_b, kmean = _moba_prep(x, positions, w["w_m"])
        nb = S // ATTN_TILE
        out_a = _attention(qT_a, k_a, vT_a, None, min(MLA_TILES, nb))
        out_b = _attention(qT_b, k_b, vT_b, kmean, min(MOBA_TILES, nb))
        x1 = _outproj_ln(out_a.reshape(B * S, -1), out_b.reshape(B * S, -1), w["wo"],
                         x.reshape(B * S, D), row(ln1_g), row(ln1_b))
        x = _ffn_ln(x1, w["wup"], w["wdn"], row(ln2_g), row(ln2_b)).reshape(B, S, D)
    return x
```

```python
import functools
import math

import numpy as np
import jax
import jax.numpy as jnp
from jax import lax
from jax.experimental import pallas as pl
from jax.experimental.pallas import tpu as pltpu

HEAD_DIM = 128
MLA_HEADS = 8
MOBA_HEADS = 8
MLA_Q_RANK = 384
MLA_KV_RANK = 256
MLA_ROPE_DIM = 64
MOBA_BLOCK = 256
MOBA_TOPK = 3
ROPE_THETA = 500000.0
PARTIAL_ROT_DIM = 32
LN_EPS = 1e-5
RMS_EPS = 1e-6
DEPTH = 1
DEEPNORM_ALPHA = (2 * DEPTH) ** 0.25
MLA_SCALE = 1.0 / math.sqrt(HEAD_DIM + MLA_ROPE_DIM)
MOBA_SCALE = 1.0 / math.sqrt(HEAD_DIM)
LOG2E = math.log2(math.e)

LANES = 128
MXU_DTYPE = jnp.bfloat16
VMEM_LIMIT_BYTES = 56 * 1024 * 1024

ATTN_TILE = MOBA_BLOCK
ATTN_HEADS = 4
MLA_TILES = 8
MOBA_TILES = 16
BF16_SUBLANES = 16
VT_ROWS = HEAD_DIM + BF16_SUBLANES
MLA_PREP_ROWS = 1024
MLA_CHAIN_ROWS = 512
MOBA_PREP_ROWS = 1024
OUT_ROWS = 512
OUT_CHUNK = 256
FFN_ROWS = 512
FFN_COLS = 1024
FFN_CHUNK = 256


def _dot(a, b):
    return jnp.dot(a, b, preferred_element_type=jnp.float32)


def _params(*sem):
    return pltpu.CompilerParams(dimension_semantics=sem, vmem_limit_bytes=VMEM_LIMIT_BYTES)


def _inv_freq(dim):
    return ROPE_THETA ** (-jnp.arange(dim // 2, dtype=jnp.float32) * (2.0 / dim))


def _rms(x, g):
    y = x * lax.rsqrt(jnp.mean(jnp.square(x), axis=-1, keepdims=True) + RMS_EPS)
    return y * g


def _store_vT(vT_ref, h, blk, v):
    vT_ref[0, h, blk, 0:HEAD_DIM, :] = v.T.astype(MXU_DTYPE)
    row = lax.broadcasted_iota(jnp.int32, (BF16_SUBLANES, ATTN_TILE), 0)
    vT_ref[0, h, blk, HEAD_DIM:VT_ROWS, :] = jnp.where(row == 0, 1.0, 0.0).astype(MXU_DTYPE)


def _mla_prep_kernel(x_ref, pos_ref, invf_ref, wlat_ref, gq_ref, gkv_ref, wuq_ref, wukv_ref,
                     qT_ref, k_ref, vT_ref):
    rows = x_ref.shape[1]
    T = ATTN_TILE
    C = min(MLA_CHAIN_ROWS, rows)
    dq = 2 * HEAD_DIM
    xb = x_ref[0].astype(MXU_DTYPE)
    lats = [_dot(xb[c0:c0 + C], wlat_ref[...]) for c0 in range(0, rows, C)]
    for ci, c0 in enumerate(range(0, rows, C)):
        r = slice(c0, c0 + C)
        lat = lats[ci]
        cq = lat[:, :MLA_Q_RANK]
        ckv = lat[:, MLA_Q_RANK:MLA_Q_RANK + MLA_KV_RANK]
        kr2 = lat[:, MLA_Q_RANK + MLA_KV_RANK:]
        qall = _dot(_rms(cq, gq_ref[...]).astype(MXU_DTYPE), wuq_ref[...])
        kvall = _dot(_rms(ckv, gkv_ref[...]).astype(MXU_DTYPE), wukv_ref[...])
        ang = pos_ref[0, r, :].astype(jnp.float32) * invf_ref[...]
        lane = lax.broadcasted_iota(jnp.int32, ang.shape, 1)
        cos, sin = jnp.cos(ang), jnp.sin(ang)
        tab = jnp.where(lane < MLA_ROPE_DIM, cos, jnp.where(lane < MLA_ROPE_DIM + MLA_ROPE_DIM // 2, -sin, sin))
        t = kr2 * tab
        k_rope = jnp.where(lane < MLA_ROPE_DIM, t + pltpu.roll(t, MLA_ROPE_DIM, 1), 0.0).astype(MXU_DTYPE)
        qall = qall * (MLA_SCALE * LOG2E)
        for h in range(MLA_HEADS):
            nope = qall[:, h * dq:h * dq + HEAD_DIM]
            t = qall[:, h * dq + HEAD_DIM:(h + 1) * dq] * tab
            rope = t + pltpu.roll(t, MLA_ROPE_DIM, 1)
            k_ref[0, h, r, 0:HEAD_DIM] = kvall[:, h * HEAD_DIM:(h + 1) * HEAD_DIM].astype(MXU_DTYPE)
            k_ref[0, h, r, HEAD_DIM:dq] = k_rope
            v = kvall[:, (MLA_HEADS + h) * HEAD_DIM:(MLA_HEADS + h + 1) * HEAD_DIM]
            for t0 in range(0, C, T):
                blk = (c0 + t0) // T
                qT_ref[0, h, blk, 0:HEAD_DIM, :] = nope[t0:t0 + T].T.astype(MXU_DTYPE)
                qT_ref[0, h, blk, HEAD_DIM:dq, :] = rope[t0:t0 + T].T.astype(MXU_DTYPE)
                _store_vT(vT_ref, h, blk, v[t0:t0 + T])


def _mla_prep(x, positions, w_lat, gq, gkv, wuq, wukv):
    B, S, D = x.shape
    rows = min(MLA_PREP_ROWS, S)
    nb = S // ATTN_TILE
    H, dq = MLA_HEADS, 2 * HEAD_DIM
    invf = jnp.tile(_inv_freq(MLA_ROPE_DIM), 2 * LANES // MLA_ROPE_DIM).reshape(1, LANES)
    full = lambda a: pl.BlockSpec(a.shape, lambda b, i: (0,) * a.ndim)
    return pl.pallas_call(
        _mla_prep_kernel,
        grid=(B, S // rows),
        in_specs=[pl.BlockSpec((1, rows, D), lambda b, i: (b, i, 0)),
                  pl.BlockSpec((1, rows, 1), lambda b, i: (b, i, 0)),
                  full(invf), full(w_lat), full(gq), full(gkv), full(wuq), full(wukv)],
        out_specs=[pl.BlockSpec((1, H, rows // ATTN_TILE, dq, ATTN_TILE), lambda b, i: (b, 0, i, 0, 0)),
                   pl.BlockSpec((1, H, rows, dq), lambda b, i: (b, 0, i, 0)),
                   pl.BlockSpec((1, H, rows // ATTN_TILE, VT_ROWS, ATTN_TILE), lambda b, i: (b, 0, i, 0, 0))],
        out_shape=[jax.ShapeDtypeStruct((B, H, nb, dq, ATTN_TILE), MXU_DTYPE),
                   jax.ShapeDtypeStruct((B, H, S, dq), MXU_DTYPE),
                   jax.ShapeDtypeStruct((B, H, nb, VT_ROWS, ATTN_TILE), MXU_DTYPE)],
        compiler_params=_params("parallel", "parallel"),
        name="mla_prep",
    )(x, positions.reshape(B, S, 1), invf, w_lat, gq, gkv, wuq, wukv)


ROT_HALF = PARTIAL_ROT_DIM // 2
ROT_X2 = LANES // 2
MOBA_HEAD_PERM = np.concatenate([np.arange(0, ROT_HALF), np.arange(PARTIAL_ROT_DIM, PARTIAL_ROT_DIM + ROT_X2 - ROT_HALF),
                                 np.arange(ROT_HALF, PARTIAL_ROT_DIM),
                                 np.arange(PARTIAL_ROT_DIM + ROT_X2 - ROT_HALF, HEAD_DIM)])


def _moba_prep_kernel(x_ref, w_ref, posc_ref, posr_ref, invr_ref, invc_ref, qT_ref, k_ref, vT_ref, kmean_ref):
    i = pl.program_id(1)
    rows = x_ref.shape[1]
    nblk = rows // ATTN_TILE
    T = ATTN_TILE
    xb = x_ref[0].astype(MXU_DTYPE)
    ang = posc_ref[0].astype(jnp.float32) * invr_ref[...]
    lane = lax.broadcasted_iota(jnp.int32, ang.shape, 1)
    k_cos = jnp.cos(ang)
    k_sin = jnp.where(lane < ROT_X2, -jnp.sin(ang), jnp.sin(ang))
    ang_t = invc_ref[...] * posr_ref[0].astype(jnp.float32)
    q_cos, q_sin = jnp.cos(ang_t), jnp.sin(ang_t)
    q_scale = MOBA_SCALE * LOG2E
    width = 2 * HEAD_DIM
    per_kind = MOBA_HEADS * HEAD_DIM // width
    for chunk in range(3 * per_kind):
        hm = _dot(xb, w_ref[:, chunk * width:(chunk + 1) * width])
        kind, pair = divmod(chunk, per_kind)
        for hh in range(2):
            h = 2 * pair + hh
            xh = hm[:, hh * HEAD_DIM:(hh + 1) * HEAD_DIM]
            if kind == 0:
                for blk in range(nblk):
                    t = xh[blk * T:(blk + 1) * T].T * q_scale
                    c, s = q_cos[:, blk * T:(blk + 1) * T], q_sin[:, blk * T:(blk + 1) * T]
                    x1, x2 = t[0:ROT_HALF], t[ROT_HALF:PARTIAL_ROT_DIM]
                    split = PARTIAL_ROT_DIM + ROT_X2 - ROT_HALF
                    qT_ref[0, h, blk, 0:ROT_HALF, :] = (x1 * c - x2 * s).astype(MXU_DTYPE)
                    qT_ref[0, h, blk, ROT_HALF:ROT_X2, :] = t[PARTIAL_ROT_DIM:split].astype(MXU_DTYPE)
                    qT_ref[0, h, blk, ROT_X2:ROT_X2 + ROT_HALF, :] = (x2 * c + x1 * s).astype(MXU_DTYPE)
                    qT_ref[0, h, blk, ROT_X2 + ROT_HALF:HEAD_DIM, :] = t[split:].astype(MXU_DTYPE)
            elif kind == 1:
                k = xh * k_cos + pltpu.roll(xh, ROT_X2, 1) * k_sin
                k_ref[0, h] = k.astype(MXU_DTYPE)
                for blk in range(nblk):
                    mean = jnp.mean(k[blk * MOBA_BLOCK:(blk + 1) * MOBA_BLOCK], axis=0, keepdims=True)
                    kmean_ref[0, h, pl.ds(i * nblk + blk, 1), :] = mean
            else:
                for blk in range(nblk):
                    _store_vT(vT_ref, h, blk, xh[blk * T:(blk + 1) * T])


def _moba_prep(x, positions, w_m):
    B, S, D = x.shape
    rows = min(MOBA_PREP_ROWS, S)
    nb = S // ATTN_TILE
    H = MOBA_HEADS
    invf = _inv_freq(PARTIAL_ROT_DIM)
    gap = jnp.zeros((ROT_X2 - ROT_HALF,), jnp.float32)
    invr = jnp.concatenate([invf, gap, invf, gap]).reshape(1, LANES)
    invc = invf.reshape(ROT_HALF, 1)
    full = lambda a: pl.BlockSpec(a.shape, lambda b, i: (0,) * a.ndim)
    tile_t = lambda d: pl.BlockSpec((1, H, rows // ATTN_TILE, d, ATTN_TILE), lambda b, i: (b, 0, i, 0, 0))
    shape_t = lambda d: jax.ShapeDtypeStruct((B, H, nb, d, ATTN_TILE), MXU_DTYPE)
    return pl.pallas_call(
        _moba_prep_kernel,
        grid=(B, S // rows),
        in_specs=[pl.BlockSpec((1, rows, D), lambda b, i: (b, i, 0)),
                  pl.BlockSpec(w_m.shape, lambda b, i: (0, 0), pipeline_mode=pl.Buffered(1)),
                  pl.BlockSpec((1, rows, 1), lambda b, i: (b, i, 0)),
                  pl.BlockSpec((1, 1, rows), lambda b, i: (b, 0, i)),
                  full(invr), full(invc)],
        out_specs=[tile_t(HEAD_DIM),
                   pl.BlockSpec((1, H, rows, HEAD_DIM), lambda b, i: (b, 0, i, 0)),
                   tile_t(VT_ROWS),
                   pl.BlockSpec((1, H, nb, HEAD_DIM), lambda b, i: (b, 0, 0, 0))],
        out_shape=[shape_t(HEAD_DIM),
                   jax.ShapeDtypeStruct((B, H, S, HEAD_DIM), MXU_DTYPE),
                   shape_t(VT_ROWS),
                   jax.ShapeDtypeStruct((B, H, nb, HEAD_DIM), jnp.float32)],
        compiler_params=_params("parallel", "arbitrary"),
        name="moba_prep",
    )(x, w_m, positions.reshape(B, S, 1), positions.reshape(B, 1, S), invr, invc)


def _attn_kernel(*refs, gated):
    if gated:
        qT_ref, k_ref, vT_ref, kmean_ref, o_ref, acc_ref, s_ref, p_ref, bias_ref = refs
    else:
        qT_ref, k_ref, vT_ref, o_ref, acc_ref, s_ref, p_ref = refs
    T = ATTN_TILE
    G, tiles = qT_ref.shape[1], qT_ref.shape[2]
    nb, dv = vT_ref.shape[2], HEAD_DIM
    base = pl.program_id(2) * tiles
    neg_inf = -jnp.inf
    heads = range(G)
    ones = tuple(jnp.ones((1, T), jnp.float32) for _ in heads)

    def scores(g, il, j):
        kb = k_ref[0, g, pl.ds(pl.multiple_of(j * T, T), T), :]
        return _dot(kb, qT_ref[0, g, il])

    def pv(g, j, p):
        return _dot(vT_ref[0, g, j], p)

    def finish(il, a_pend, j_pend):
        accs = [a_pend[g] * acc_ref[g] + pv(g, j_pend, p_ref[g]) for g in heads]
        rows = pl.ds(pl.multiple_of(il * T, T), T)
        for g in heads:
            inv = 1.0 / accs[g][dv:dv + 1]
            o_ref[0, rows, g * dv:(g + 1) * dv] = (accs[g][0:dv] * inv).T.astype(o_ref.dtype)

    def start(il, i):
        if gated:
            gates = [_dot(kmean_ref[0, g].astype(MXU_DTYPE), qT_ref[0, g, il]) for g in heads]
        s_own = [scores(g, il, i) for g in heads]
        cms = []
        for g in heads:
            s_first = scores(g, il, 0)
            s_ref[0, g] = s_first
            cms.append(jnp.max(s_first, axis=0, keepdims=True))
        if gated:
            for g in heads:
                row = lax.broadcasted_iota(jnp.int32, gates[g].shape, 0)
                past = row < i
                gate = jnp.where(past, gates[g], neg_inf)
                taken = row < 0
                for _ in range(min(MOBA_TOPK, nb)):
                    cand = jnp.where(taken, neg_inf, gate)
                    best = jnp.max(cand, axis=0, keepdims=True)
                    hit = (cand == best) & jnp.logical_not(taken)
                    first = jnp.min(jnp.where(hit, row, nb), axis=0, keepdims=True)
                    taken = taken | (row == first)
                bias_ref[g] = jnp.where(taken & past, 0.0, neg_inf)
        kpos = lax.broadcasted_iota(jnp.int32, (T, T), 0)
        qpos = lax.broadcasted_iota(jnp.int32, (T, T), 1)
        causal = kpos <= qpos
        ms = []
        for g in heads:
            s = jnp.where(causal, s_own[g], neg_inf)
            m = jnp.max(s, axis=0, keepdims=True)
            ms.append(m)
            p_ref[g] = jnp.exp2(s - m).astype(MXU_DTYPE)
            acc_ref[g] = jnp.zeros(acc_ref.shape[1:], jnp.float32)
        return tuple(ms), tuple(cms)

    def stage(il, t, slot, carry, produce_next=True):
        a_pend, ms, cms, j_pend = carry
        for g in heads:
            acc_ref[g] = a_pend[g] * acc_ref[g] + pv(g, j_pend, p_ref[g])
        new_cms = []
        if produce_next:
            nxt = jnp.minimum(t + 1, nb - 1)
            for g in heads:
                s_next = scores(g, il, nxt)
                s_ref[1 - slot, g] = s_next
                new_cms.append(jnp.max(s_next, axis=0, keepdims=True))
        else:
            new_cms = cms
        new_ms, new_as = [], []
        for g in heads:
            if gated:
                b = bias_ref[g, pl.ds(t, 1), :]
                m_new = jnp.maximum(ms[g], cms[g] + b)
                p = jnp.exp2(s_ref[slot, g] - (m_new - b))
            else:
                m_new = jnp.maximum(ms[g], cms[g])
                p = jnp.exp2(s_ref[slot, g] - m_new)
            new_as.append(jnp.exp2(ms[g] - m_new))
            new_ms.append(m_new)
            p_ref[g] = p.astype(MXU_DTYPE)
        return tuple(new_as), tuple(new_ms), tuple(new_cms), t

    for g in heads:
        acc_ref[g] = jnp.ones(acc_ref.shape[1:], jnp.float32)
        p_ref[g] = jnp.zeros(p_ref.shape[1:], MXU_DTYPE)

    def tile(il, carry):
        a_pend, j_pend = carry
        finish(jnp.maximum(il - 1, 0), a_pend, j_pend)
        cur = jnp.minimum(il, tiles - 1)
        i = base + cur
        ms, cms = start(cur, i)

        def pair(t0, c):
            return stage(cur, t0 + 1, 1, stage(cur, t0, 0, c))

        def quad(t0, c):
            return pair(t0 + 2, pair(t0, c))

        n = jnp.where(il < tiles, i, 0)
        pairs = n // 2
        c = lax.fori_loop(0, pairs // 4, lambda u, c: quad(8 * u + 4, quad(8 * u, c)), (ones, ms, cms, i))
        c = lax.fori_loop(0, (pairs % 4) // 2, lambda u, c: quad(8 * (pairs // 4), c), c)
        c = lax.fori_loop(0, pairs % 2, lambda u, c: pair(4 * (pairs // 2), c), c)
        a_pend, _, _, j_pend = lax.fori_loop(
            0, n % 2, lambda u, c: stage(cur, n - 1, 0, c, produce_next=False), c)
        return a_pend, j_pend

    lax.fori_loop(0, tiles + 1, tile, (ones, jnp.int32(0)))


def _attention(qT, k, vT, kmean, tiles):
    B, H, nb, d, T = qT.shape
    S = nb * T
    dv = HEAD_DIM
    G = ATTN_HEADS
    gated = kmean is not None
    in_specs = [pl.BlockSpec((1, G, tiles, d, T), lambda b, h, i: (b, h, i, 0, 0)),
                pl.BlockSpec((1, G, S, d), lambda b, h, i: (b, h, 0, 0)),
                pl.BlockSpec((1, G, nb, VT_ROWS, T), lambda b, h, i: (b, h, 0, 0, 0))]
    args = [qT, k, vT]
    scratch = [pltpu.VMEM((G, VT_ROWS, T), jnp.float32),
               pltpu.VMEM((2, G, T, T), jnp.float32),
               pltpu.VMEM((G, T, T), MXU_DTYPE)]
    if gated:
        in_specs.append(pl.BlockSpec((1, G, nb, d), lambda b, h, i: (b, h, 0, 0)))
        args.append(kmean)
        scratch.append(pltpu.VMEM((G, nb, T), jnp.float32))
    return pl.pallas_call(
        functools.partial(_attn_kernel, gated=gated),
        grid=(B, H // G, nb // tiles),
        in_specs=in_specs,
        out_specs=pl.BlockSpec((1, tiles * T, G * dv), lambda b, h, i: (b, i, h)),
        out_shape=jax.ShapeDtypeStruct((B, S, H * dv), MXU_DTYPE),
        scratch_shapes=scratch,
        compiler_params=_params("parallel", "parallel", "parallel"),
        name="moba_attention" if gated else "mla_attention",
    )(*args)


def _layer_norm(y, g, b):
    mu = jnp.mean(y, axis=-1, keepdims=True)
    yc = y - mu
    var = jnp.mean(jnp.square(yc), axis=-1, keepdims=True)
    return yc * lax.rsqrt(var + LN_EPS) * g + b


def _outproj_kernel(a1_ref, a2_ref, w_ref, x_ref, g_ref, b_ref, o_ref):
    step = OUT_CHUNK
    k1 = a1_ref.shape[1]
    for r0 in range(0, x_ref.shape[0], step):
        r = slice(r0, r0 + step)
        mix = _dot(a1_ref[r, :], w_ref[0:k1, :]) + _dot(a2_ref[r, :], w_ref[k1:, :])
        o_ref[r, :] = _layer_norm(DEEPNORM_ALPHA * x_ref[r, :] + mix, g_ref[...], b_ref[...])


def _outproj_ln(a1, a2, w, x, g, b):
    M, D = x.shape
    rows = min(OUT_ROWS, M)
    K = a1.shape[1]
    full = lambda a: pl.BlockSpec(a.shape, lambda i: (0,) * a.ndim)
    return pl.pallas_call(
        _outproj_kernel,
        grid=(M // rows,),
        in_specs=[pl.BlockSpec((rows, K), lambda i: (i, 0)), pl.BlockSpec((rows, K), lambda i: (i, 0)),
                  pl.BlockSpec(w.shape, lambda i: (0, 0), pipeline_mode=pl.Buffered(1)),
                  pl.BlockSpec((rows, D), lambda i: (i, 0)), full(g), full(b)],
        out_specs=pl.BlockSpec((rows, D), lambda i: (i, 0)),
        out_shape=jax.ShapeDtypeStruct((M, D), jnp.float32),
        compiler_params=_params("parallel"),
        name="outproj_ln1",
    )(a1, a2, w, x, g, b)


def _ffn_kernel(x_ref, wup_ref, wdn_ref, g_ref, b_ref, o_ref, xb_ref, acc_ref):
    f = pl.program_id(1)
    last = pl.num_programs(1) - 1

    def partial(r):
        u = jnp.maximum(_dot(xb_ref[r, :], wup_ref[...]), 0.0)
        return _dot(jnp.square(u).astype(MXU_DTYPE), wdn_ref[...])

    @pl.when(f == 0)
    def _():
        xb_ref[...] = x_ref[...].astype(MXU_DTYPE)
        acc_ref[...] = partial(slice(None))

    @pl.when((f > 0) & (f < last))
    def _():
        acc_ref[...] += partial(slice(None))

    @pl.when(f == last)
    def _():
        for r0 in range(0, x_ref.shape[0], FFN_CHUNK):
            r = slice(r0, r0 + FFN_CHUNK)
            y = acc_ref[r, :] + partial(r)
            o_ref[r, :] = _layer_norm(DEEPNORM_ALPHA * x_ref[r, :] + y, g_ref[...], b_ref[...])


def _ffn_ln(x, wup, wdn, g, b):
    M, D = x.shape
    F = wup.shape[1]
    rows = min(FFN_ROWS, M)
    cols = min(FFN_COLS, F)
    vec = pl.BlockSpec((1, D), lambda i, f: (0, 0))
    return pl.pallas_call(
        _ffn_kernel,
        grid=(M // rows, F // cols),
        in_specs=[pl.BlockSpec((rows, D), lambda i, f: (i, 0)),
                  pl.BlockSpec((D, cols), lambda i, f: (0, f)),
                  pl.BlockSpec((cols, D), lambda i, f: (f, 0)),
                  vec, vec],
        out_specs=pl.BlockSpec((rows, D), lambda i, f: (i, 0)),
        out_shape=jax.ShapeDtypeStruct((M, D), jnp.float32),
        scratch_shapes=[pltpu.VMEM((rows, D), MXU_DTYPE), pltpu.VMEM((rows, D), jnp.float32)],
        compiler_params=_params("parallel", "arbitrary"),
        name="ffn_ln2",
    )(x, wup, wdn, g, b)


def _layer_weights(w_in, w_uq, w_ukv, w_out, w_up, w_down):
    D = w_in.shape[0]
    half = MLA_ROPE_DIM // 2
    swap = np.concatenate([np.arange(half, MLA_ROPE_DIM), np.arange(half)])
    c0 = MLA_Q_RANK + MLA_KV_RANK
    c1 = c0 + MLA_ROPE_DIM
    mw = MOBA_HEADS * HEAD_DIM
    cast = lambda w: w.astype(MXU_DTYPE)
    w_in, w_uq, w_ukv = cast(w_in), cast(w_uq), cast(w_ukv)
    w_lat = jnp.concatenate([w_in[:, :c1], w_in[:, c0:c1][:, swap]], axis=1)
    k_cols = w_in[:, c1 + mw:c1 + 2 * mw].reshape(D, MOBA_HEADS, HEAD_DIM)[:, :, MOBA_HEAD_PERM].reshape(D, mw)
    w_m = jnp.concatenate([w_in[:, c1:c1 + mw], k_cols, w_in[:, c1 + 2 * mw:c1 + 3 * mw]], axis=1)
    uq = w_uq.reshape(MLA_Q_RANK, MLA_HEADS, HEAD_DIM + MLA_ROPE_DIM)
    rope_cols = uq[:, :, HEAD_DIM:]
    uq = jnp.concatenate([uq, rope_cols[:, :, swap]], axis=-1).reshape(MLA_Q_RANK, MLA_HEADS * 2 * HEAD_DIM)
    ukv = w_ukv.reshape(MLA_KV_RANK, MLA_HEADS, 2, HEAD_DIM).transpose(0, 2, 1, 3).reshape(MLA_KV_RANK, -1)
    return dict(w_lat=w_lat, w_m=w_m, uq=uq, ukv=ukv, wo=cast(w_out), wup=cast(w_up), wdn=cast(w_down))


def kernel(x, positions, w_in, mla_q_norm, mla_kv_norm, w_uq, w_ukv, w_out, ln1_g, ln1_b, w_up, w_down, ln2_g, ln2_b):
    B, S, D = x.shape
    assert S % ATTN_TILE == 0 and w_in.shape[0] == DEPTH
    for l in range(DEPTH):
        w = _layer_weights(w_in[l], w_uq[l], w_ukv[l], w_out[l], w_up[l], w_down[l])
        row = lambda v: v[l].reshape(1, -1)
        qT_a, k_a, vT_a = _mla_prep(x, positions, w["w_lat"], row(mla_q_norm), row(mla_kv_norm), w["uq"], w["ukv"])
        qT_b, k_b, vT_b, kmean = _moba_prep(x, positions, w["w_m"])
        nb = S // ATTN_TILE
        out_a = _attention(qT_a, k_a, vT_a, None, min(MLA_TILES, nb))
        out_b = _attention(qT_b, k_b, vT_b, kmean, min(MOBA_TILES, nb))
        x1 = _outproj_ln(out_a.reshape(B * S, -1), out_b.reshape(B * S, -1), w["wo"],
                         x.reshape(B * S, D), row(ln1_g), row(ln1_b))
        x = _ffn_ln(x1, w["wup"], w["wdn"], row(ln2_g), row(ln2_b)).reshape(B, S, D)
    return x
```

```python
import functools
import math

import numpy as np
import jax
import jax.numpy as jnp
from jax import lax
from jax.experimental import pallas as pl
from jax.experimental.pallas import tpu as pltpu

HEAD_DIM = 128
MLA_HEADS = 8
MOBA_HEADS = 8
MLA_Q_RANK = 384
MLA_KV_RANK = 256
MLA_ROPE_DIM = 64
MOBA_BLOCK = 256
MOBA_TOPK = 3
ROPE_THETA = 500000.0
PARTIAL_ROT_DIM = 32
LN_EPS = 1e-5
RMS_EPS = 1e-6
DEPTH = 1
DEEPNORM_ALPHA = (2 * DEPTH) ** 0.25
MLA_SCALE = 1.0 / math.sqrt(HEAD_DIM + MLA_ROPE_DIM)
MOBA_SCALE = 1.0 / math.sqrt(HEAD_DIM)
LOG2E = math.log2(math.e)

LANES = 128
MXU_DTYPE = jnp.bfloat16
VMEM_LIMIT_BYTES = 56 * 1024 * 1024

ATTN_TILE = MOBA_BLOCK
ATTN_HEADS = 4
MLA_TILES = 8
MOBA_TILES = 16
BF16_SUBLANES = 16
VT_ROWS = HEAD_DIM + BF16_SUBLANES
MLA_PREP_ROWS = 1024
MLA_CHAIN_ROWS = 512
MOBA_PREP_ROWS = 1024
OUT_ROWS = 512
OUT_CHUNK = 256
FFN_ROWS = 512
FFN_COLS = 1024
FFN_CHUNK = 256


def _dot(a, b):
    return jnp.dot(a, b, preferred_element_type=jnp.float32)


def _params(*sem):
    return pltpu.CompilerParams(dimension_semantics=sem, vmem_limit_bytes=VMEM_LIMIT_BYTES)


def _inv_freq(dim):
    return ROPE_THETA ** (-jnp.arange(dim // 2, dtype=jnp.float32) * (2.0 / dim))


def _rms(x, g):
    y = x * lax.rsqrt(jnp.mean(jnp.square(x), axis=-1, keepdims=True) + RMS_EPS)
    return y * g


def _store_vT(vT_ref, h, blk, v):
    vT_ref[0, h, blk, 0:HEAD_DIM, :] = v.T.astype(MXU_DTYPE)
    row = lax.broadcasted_iota(jnp.int32, (BF16_SUBLANES, ATTN_TILE), 0)
    vT_ref[0, h, blk, HEAD_DIM:VT_ROWS, :] = jnp.where(row == 0, 1.0, 0.0).astype(MXU_DTYPE)


def _mla_prep_kernel(x_ref, pos_ref, invf_ref, wlat_ref, gq_ref, gkv_ref, wuq_ref, wukv_ref,
                     qT_ref, k_ref, vT_ref):
    rows = x_ref.shape[1]
    T = ATTN_TILE
    C = min(MLA_CHAIN_ROWS, rows)
    dq = 2 * HEAD_DIM
    xb = x_ref[0].astype(MXU_DTYPE)
    lats = [_dot(xb[c0:c0 + C], wlat_ref[...]) for c0 in range(0, rows, C)]
    for ci, c0 in enumerate(range(0, rows, C)):
        r = slice(c0, c0 + C)
        lat = lats[ci]
        cq = lat[:, :MLA_Q_RANK]
        ckv = lat[:, MLA_Q_RANK:MLA_Q_RANK + MLA_KV_RANK]
        kr2 = lat[:, MLA_Q_RANK + MLA_KV_RANK:]
        qall = _dot(_rms(cq, gq_ref[...]).astype(MXU_DTYPE), wuq_ref[...])
        kvall = _dot(_rms(ckv, gkv_ref[...]).astype(MXU_DTYPE), wukv_ref[...])
        ang = pos_ref[0, r, :].astype(jnp.float32) * invf_ref[...]
        lane = lax.broadcasted_iota(jnp.int32, ang.shape, 1)
        cos, sin = jnp.cos(ang), jnp.sin(ang)
        tab = jnp.where(lane < MLA_ROPE_DIM, cos, jnp.where(lane < MLA_ROPE_DIM + MLA_ROPE_DIM // 2, -sin, sin))
        t = kr2 * tab
        k_rope = jnp.where(lane < MLA_ROPE_DIM, t + pltpu.roll(t, MLA_ROPE_DIM, 1), 0.0).astype(MXU_DTYPE)
        qall = qall * (MLA_SCALE * LOG2E)
        for h in range(MLA_HEADS):
            nope = qall[:, h * dq:h * dq + HEAD_DIM]
            t = qall[:, h * dq + HEAD_DIM:(h + 1) * dq] * tab
            rope = t + pltpu.roll(t, MLA_ROPE_DIM, 1)
            k_ref[0, h, r, 0:HEAD_DIM] = kvall[:, h * HEAD_DIM:(h + 1) * HEAD_DIM].astype(MXU_DTYPE)
            k_ref[0, h, r, HEAD_DIM:dq] = k_rope
            v = kvall[:, (MLA_HEADS + h) * HEAD_DIM:(MLA_HEADS + h + 1) * HEAD_DIM]
            for t0 in range(0, C, T):
                blk = (c0 + t0) // T
                qT_ref[0, h, blk, 0:HEAD_DIM, :] = nope[t0:t0 + T].T.astype(MXU_DTYPE)
                qT_ref[0, h, blk, HEAD_DIM:dq, :] = rope[t0:t0 + T].T.astype(MXU_DTYPE)
                _store_vT(vT_ref, h, blk, v[t0:t0 + T])


def _mla_prep(x, positions, w_lat, gq, gkv, wuq, wukv):
    B, S, D = x.shape
    rows = min(MLA_PREP_ROWS, S)
    nb = S // ATTN_TILE
    H, dq = MLA_HEADS, 2 * HEAD_DIM
    invf = jnp.tile(_inv_freq(MLA_ROPE_DIM), 2 * LANES // MLA_ROPE_DIM).reshape(1, LANES)
    full = lambda a: pl.BlockSpec(a.shape, lambda b, i: (0,) * a.ndim)
    return pl.pallas_call(
        _mla_prep_kernel,
        grid=(B, S // rows),
        in_specs=[pl.BlockSpec((1, rows, D), lambda b, i: (b, i, 0)),
                  pl.BlockSpec((1, rows, 1), lambda b, i: (b, i, 0)),
                  full(invf), full(w_lat), full(gq), full(gkv), full(wuq), full(wukv)],
        out_specs=[pl.BlockSpec((1, H, rows // ATTN_TILE, dq, ATTN_TILE), lambda b, i: (b, 0, i, 0, 0)),
                   pl.BlockSpec((1, H, rows, dq), lambda b, i: (b, 0, i, 0)),
                   pl.BlockSpec((1, H, rows // ATTN_TILE, VT_ROWS, ATTN_TILE), lambda b, i: (b, 0, i, 0, 0))],
        out_shape=[jax.ShapeDtypeStruct((B, H, nb, dq, ATTN_TILE), MXU_DTYPE),
                   jax.ShapeDtypeStruct((B, H, S, dq), MXU_DTYPE),
                   jax.ShapeDtypeStruct((B, H, nb, VT_ROWS, ATTN_TILE), MXU_DTYPE)],
        compiler_params=_params("parallel", "parallel"),
        name="mla_prep",
    )(x, positions.reshape(B, S, 1), invf, w_lat, gq, gkv, wuq, wukv)


ROT_HALF = PARTIAL_ROT_DIM // 2
ROT_X2 = LANES // 2
MOBA_HEAD_PERM = np.concatenate([np.arange(0, ROT_HALF), np.arange(PARTIAL_ROT_DIM, PARTIAL_ROT_DIM + ROT_X2 - ROT_HALF),
                                 np.arange(ROT_HALF, PARTIAL_ROT_DIM),
                                 np.arange(PARTIAL_ROT_DIM + ROT_X2 - ROT_HALF, HEAD_DIM)])


def _moba_prep_kernel(x_ref, w_ref, posc_ref, posr_ref, invr_ref, invc_ref, qT_ref, k_ref, vT_ref, kmean_ref):
    i = pl.program_id(1)
    rows = x_ref.shape[1]
    nblk = rows // ATTN_TILE
    T = ATTN_TILE
    xb = x_ref[0].astype(MXU_DTYPE)
    ang = posc_ref[0].astype(jnp.float32) * invr_ref[...]
    lane = lax.broadcasted_iota(jnp.int32, ang.shape, 1)
    k_cos = jnp.cos(ang)
    k_sin = jnp.where(lane < ROT_X2, -jnp.sin(ang), jnp.sin(ang))
    ang_t = invc_ref[...] * posr_ref[0].astype(jnp.float32)
    q_cos, q_sin = jnp.cos(ang_t), jnp.sin(ang_t)
    q_scale = MOBA_SCALE * LOG2E
    width = 2 * HEAD_DIM
    per_kind = MOBA_HEADS * HEAD_DIM // width
    for chunk in range(3 * per_kind):
        hm = _dot(xb, w_ref[:, chunk * width:(chunk + 1) * width])
        kind, pair = divmod(chunk, per_kind)
        for hh in range(2):
            h = 2 * pair + hh
            xh = hm[:, hh * HEAD_DIM:(hh + 1) * HEAD_DIM]
            if kind == 0:
                for blk in range(nblk):
                    t = xh[blk * T:(blk + 1) * T].T * q_scale
                    c, s = q_cos[:, blk * T:(blk + 1) * T], q_sin[:, blk * T:(blk + 1) * T]
                    x1, x2 = t[0:ROT_HALF], t[ROT_HALF:PARTIAL_ROT_DIM]
                    split = PARTIAL_ROT_DIM + ROT_X2 - ROT_HALF
                    qT_ref[0, h, blk, 0:ROT_HALF, :] = (x1 * c - x2 * s).astype(MXU_DTYPE)
                    qT_ref[0, h, blk, ROT_HALF:ROT_X2, :] = t[PARTIAL_ROT_DIM:split].astype(MXU_DTYPE)
                    qT_ref[0, h, blk, ROT_X2:ROT_X2 + ROT_HALF, :] = (x2 * c + x1 * s).astype(MXU_DTYPE)
                    qT_ref[0, h, blk, ROT_X2 + ROT_HALF:HEAD_DIM, :] = t[split:].astype(MXU_DTYPE)
            elif kind == 1:
                k = xh * k_cos + pltpu.roll(xh, ROT_X2, 1) * k_sin
                k_ref[0, h] = k.astype(MXU_DTYPE)
                for blk in range(nblk):
                    mean = jnp.mean(k[blk * MOBA_BLOCK:(blk + 1) * MOBA_BLOCK], axis=0, keepdims=True)
                    kmean_ref[0, h, pl.ds(i * nblk + blk, 1), :] = mean
            else:
                for blk in range(nblk):
                    _store_vT(vT_ref, h, blk, xh[blk * T:(blk + 1) * T])


def _moba_prep(x, positions, w_m):
    B, S, D = x.shape
    rows = min(MOBA_PREP_ROWS, S)
    nb = S // ATTN_TILE
    H = MOBA_HEADS
    invf = _inv_freq(PARTIAL_ROT_DIM)
    gap = jnp.zeros((ROT_X2 - ROT_HALF,), jnp.float32)
    invr = jnp.concatenate([invf, gap, invf, gap]).reshape(1, LANES)
    invc = invf.reshape(ROT_HALF, 1)
    full = lambda a: pl.BlockSpec(a.shape, lambda b, i: (0,) * a.ndim)
    tile_t = lambda d: pl.BlockSpec((1, H, rows // ATTN_TILE, d, ATTN_TILE), lambda b, i: (b, 0, i, 0, 0))
    shape_t = lambda d: jax.ShapeDtypeStruct((B, H, nb, d, ATTN_TILE), MXU_DTYPE)
    return pl.pallas_call(
        _moba_prep_kernel,
        grid=(B, S // rows),
        in_specs=[pl.BlockSpec((1, rows, D), lambda b, i: (b, i, 0)),
                  pl.BlockSpec(w_m.shape, lambda b, i: (0, 0), pipeline_mode=pl.Buffered(1)),
                  pl.BlockSpec((1, rows, 1), lambda b, i: (b, i, 0)),
                  pl.BlockSpec((1, 1, rows), lambda b, i: (b, 0, i)),
                  full(invr), full(invc)],
        out_specs=[tile_t(HEAD_DIM),
                   pl.BlockSpec((1, H, rows, HEAD_DIM), lambda b, i: (b, 0, i, 0)),
                   tile_t(VT_ROWS),
                   pl.BlockSpec((1, H, nb, HEAD_DIM), lambda b, i: (b, 0, 0, 0))],
        out_shape=[shape_t(HEAD_DIM),
                   jax.ShapeDtypeStruct((B, H, S, HEAD_DIM), MXU_DTYPE),
                   shape_t(VT_ROWS),
                   jax.ShapeDtypeStruct((B, H, nb, HEAD_DIM), jnp.float32)],
        compiler_params=_params("parallel", "arbitrary"),
        name="moba_prep",
    )(x, w_m, positions.reshape(B, S, 1), positions.reshape(B, 1, S), invr, invc)


def _attn_kernel(*refs, gated):
    if gated:
        qT_ref, k_ref, vT_ref, kmean_ref, o_ref, acc_ref, s_ref, p_ref, bias_ref = refs
    else:
        qT_ref, k_ref, vT_ref, o_ref, acc_ref, s_ref, p_ref = refs
    T = ATTN_TILE
    G, tiles = qT_ref.shape[1], qT_ref.shape[2]
    nb, dv = vT_ref.shape[2], HEAD_DIM
    base = pl.program_id(2) * tiles
    neg_inf = -jnp.inf
    heads = range(G)
    ones = tuple(jnp.ones((1, T), jnp.float32) for _ in heads)

    def scores(g, il, j):
        kb = k_ref[0, g, pl.ds(pl.multiple_of(j * T, T), T), :]
        return _dot(kb, qT_ref[0, g, il])

    def pv(g, j, p):
        return _dot(vT_ref[0, g, j], p)

    def finish(il, a_pend, j_pend):
        accs = [a_pend[g] * acc_ref[g] + pv(g, j_pend, p_ref[g]) for g in heads]
        rows = pl.ds(pl.multiple_of(il * T, T), T)
        for g in heads:
            inv = 1.0 / accs[g][dv:dv + 1]
            o_ref[0, rows, g * dv:(g + 1) * dv] = (accs[g][0:dv] * inv).T.astype(o_ref.dtype)

    def start(il, i):
        if gated:
            gates = [_dot(kmean_ref[0, g].astype(MXU_DTYPE), qT_ref[0, g, il]) for g in heads]
        s_own = [scores(g, il, i) for g in heads]
        cms = []
        for g in heads:
            s_first = scores(g, il, 0)
            s_ref[0, g] = s_first
            cms.append(jnp.max(s_first, axis=0, keepdims=True))
        if gated:
            for g in heads:
                row = lax.broadcasted_iota(jnp.int32, gates[g].shape, 0)
                past = row < i
                gate = jnp.where(past, gates[g], neg_inf)
                taken = row < 0
                for _ in range(min(MOBA_TOPK, nb)):
                    cand = jnp.where(taken, neg_inf, gate)
                    best = jnp.max(cand, axis=0, keepdims=True)
                    hit = (cand == best) & jnp.logical_not(taken)
                    first = jnp.min(jnp.where(hit, row, nb), axis=0, keepdims=True)
                    taken = taken | (row == first)
                bias_ref[g] = jnp.where(taken & past, 0.0, neg_inf)
        kpos = lax.broadcasted_iota(jnp.int32, (T, T), 0)
        qpos = lax.broadcasted_iota(jnp.int32, (T, T), 1)
        causal = kpos <= qpos
        ms = []
        for g in heads:
            s = jnp.where(causal, s_own[g], neg_inf)
            m = jnp.max(s, axis=0, keepdims=True)
            ms.append(m)
            p_ref[g] = jnp.exp2(s - m).astype(MXU_DTYPE)
            acc_ref[g] = jnp.zeros(acc_ref.shape[1:], jnp.float32)
        return tuple(ms), tuple(cms)

    def stage(il, t, slot, carry, produce_next=True):
        a_pend, ms, cms, j_pend = carry
        for g in heads:
            acc_ref[g] = a_pend[g] * acc_ref[g] + pv(g, j_pend, p_ref[g])
        new_cms = []
        if produce_next:
            nxt = jnp.minimum(t + 1, nb - 1)
            for g in heads:
                s_next = scores(g, il, nxt)
                s_ref[1 - slot, g] = s_next
                new_cms.append(jnp.max(s_next, axis=0, keepdims=True))
        else:
            new_cms = cms
        new_ms, new_as = [], []
        for g in heads:
            if gated:
                b = bias_ref[g, pl.ds(t, 1), :]
                m_new = jnp.maximum(ms[g], cms[g] + b)
                shift = m_new - b
            else:
                m_new = jnp.maximum(ms[g], cms[g])
                shift = m_new
            new_as.append(jnp.exp2(ms[g] - m_new))
            new_ms.append(m_new)
            for q0 in range(0, T, LANES):
                q = slice(q0, q0 + LANES)
                p_ref[g, :, q] = jnp.exp2(s_ref[slot, g, :, q] - shift[:, q]).astype(MXU_DTYPE)
        return tuple(new_as), tuple(new_ms), tuple(new_cms), t

    for g in heads:
        acc_ref[g] = jnp.ones(acc_ref.shape[1:], jnp.float32)
        p_ref[g] = jnp.zeros(p_ref.shape[1:], MXU_DTYPE)

    def tile(il, carry):
        a_pend, j_pend = carry
        finish(jnp.maximum(il - 1, 0), a_pend, j_pend)
        cur = jnp.minimum(il, tiles - 1)
        i = base + cur
        ms, cms = start(cur, i)

        def pair(t0, c):
            return stage(cur, t0 + 1, 1, stage(cur, t0, 0, c))

        def quad(t0, c):
            return pair(t0 + 2, pair(t0, c))

        n = jnp.where(il < tiles, i, 0)
        pairs = n // 2
        c = lax.fori_loop(0, pairs // 4, lambda u, c: quad(8 * u + 4, quad(8 * u, c)), (ones, ms, cms, i))
        c = lax.fori_loop(0, (pairs % 4) // 2, lambda u, c: quad(8 * (pairs // 4), c), c)
        c = lax.fori_loop(0, pairs % 2, lambda u, c: pair(4 * (pairs // 2), c), c)
        a_pend, _, _, j_pend = lax.fori_loop(
            0, n % 2, lambda u, c: stage(cur, n - 1, 0, c, produce_next=False), c)
        return a_pend, j_pend

    lax.fori_loop(0, tiles + 1, tile, (ones, jnp.int32(0)))


def _attention(qT, k, vT, kmean, tiles):
    B, H, nb, d, T = qT.shape
    S = nb * T
    dv = HEAD_DIM
    G = ATTN_HEADS
    gated = kmean is not None
    in_specs = [pl.BlockSpec((1, G, tiles, d, T), lambda b, h, i: (b, h, i, 0, 0)),
                pl.BlockSpec((1, G, S, d), lambda b, h, i: (b, h, 0, 0)),
                pl.BlockSpec((1, G, nb, VT_ROWS, T), lambda b, h, i: (b, h, 0, 0, 0))]
    args = [qT, k, vT]
    scratch = [pltpu.VMEM((G, VT_ROWS, T), jnp.float32),
               pltpu.VMEM((2, G, T, T), jnp.float32),
               pltpu.VMEM((G, T, T), MXU_DTYPE)]
    if gated:
        in_specs.append(pl.BlockSpec((1, G, nb, d), lambda b, h, i: (b, h, 0, 0)))
        args.append(kmean)
        scratch.append(pltpu.VMEM((G, nb, T), jnp.float32))
    return pl.pallas_call(
        functools.partial(_attn_kernel, gated=gated),
        grid=(B, H // G, nb // tiles),
        in_specs=in_specs,
        out_specs=pl.BlockSpec((1, tiles * T, G * dv), lambda b, h, i: (b, i, h)),
        out_shape=jax.ShapeDtypeStruct((B, S, H * dv), MXU_DTYPE),
        scratch_shapes=scratch,
        compiler_params=_params("parallel", "parallel", "parallel"),
        name="moba_attention" if gated else "mla_attention",
    )(*args)


def _layer_norm(y, g, b):
    mu = jnp.mean(y, axis=-1, keepdims=True)
    yc = y - mu
    var = jnp.mean(jnp.square(yc), axis=-1, keepdims=True)
    return yc * lax.rsqrt(var + LN_EPS) * g + b


def _outproj_kernel(a1_ref, a2_ref, w_ref, x_ref, g_ref, b_ref, o_ref):
    step = OUT_CHUNK
    k1 = a1_ref.shape[1]
    for r0 in range(0, x_ref.shape[0], step):
        r = slice(r0, r0 + step)
        mix = _dot(a1_ref[r, :], w_ref[0:k1, :]) + _dot(a2_ref[r, :], w_ref[k1:, :])
        o_ref[r, :] = _layer_norm(DEEPNORM_ALPHA * x_ref[r, :] + mix, g_ref[...], b_ref[...])


def _outproj_ln(a1, a2, w, x, g, b):
    M, D = x.shape
    rows = min(OUT_ROWS, M)
    K = a1.shape[1]
    full = lambda a: pl.BlockSpec(a.shape, lambda i: (0,) * a.ndim)
    return pl.pallas_call(
        _outproj_kernel,
        grid=(M // rows,),
        in_specs=[pl.BlockSpec((rows, K), lambda i: (i, 0)), pl.BlockSpec((rows, K), lambda i: (i, 0)),
                  pl.BlockSpec(w.shape, lambda i: (0, 0), pipeline_mode=pl.Buffered(1)),
                  pl.BlockSpec((rows, D), lambda i: (i, 0)), full(g), full(b)],
        out_specs=pl.BlockSpec((rows, D), lambda i: (i, 0)),
        out_shape=jax.ShapeDtypeStruct((M, D), jnp.float32),
        compiler_params=_params("parallel"),
        name="outproj_ln1",
    )(a1, a2, w, x, g, b)


def _ffn_kernel(x_ref, wup_ref, wdn_ref, g_ref, b_ref, o_ref, xb_ref, acc_ref):
    f = pl.program_id(1)
    last = pl.num_programs(1) - 1

    def partial(r):
        u = jnp.maximum(_dot(xb_ref[r, :], wup_ref[...]), 0.0)
        return _dot(jnp.square(u).astype(MXU_DTYPE), wdn_ref[...])

    @pl.when(f == 0)
    def _():
        xb_ref[...] = x_ref[...].astype(MXU_DTYPE)
        acc_ref[...] = partial(slice(None))

    @pl.when((f > 0) & (f < last))
    def _():
        acc_ref[...] += partial(slice(None))

    @pl.when(f == last)
    def _():
        for r0 in range(0, x_ref.shape[0], FFN_CHUNK):
            r = slice(r0, r0 + FFN_CHUNK)
            y = acc_ref[r, :] + partial(r)
            o_ref[r, :] = _layer_norm(DEEPNORM_ALPHA * x_ref[r, :] + y, g_ref[...], b_ref[...])


def _ffn_ln(x, wup, wdn, g, b):
    M, D = x.shape
    F = wup.shape[1]
    rows = min(FFN_ROWS, M)
    cols = min(FFN_COLS, F)
    vec = pl.BlockSpec((1, D), lambda i, f: (0, 0))
    return pl.pallas_call(
        _ffn_kernel,
        grid=(M // rows, F // cols),
        in_specs=[pl.BlockSpec((rows, D), lambda i, f: (i, 0)),
                  pl.BlockSpec((D, cols), lambda i, f: (0, f)),
                  pl.BlockSpec((cols, D), lambda i, f: (f, 0)),
                  vec, vec],
        out_specs=pl.BlockSpec((rows, D), lambda i, f: (i, 0)),
        out_shape=jax.ShapeDtypeStruct((M, D), jnp.float32),
        scratch_shapes=[pltpu.VMEM((rows, D), MXU_DTYPE), pltpu.VMEM((rows, D), jnp.float32)],
        compiler_params=_params("parallel", "arbitrary"),
        name="ffn_ln2",
    )(x, wup, wdn, g, b)


def _layer_weights(w_in, w_uq, w_ukv, w_out, w_up, w_down):
    D = w_in.shape[0]
    half = MLA_ROPE_DIM // 2
    swap = np.concatenate([np.arange(half, MLA_ROPE_DIM), np.arange(half)])
    c0 = MLA_Q_RANK + MLA_KV_RANK
    c1 = c0 + MLA_ROPE_DIM
    mw = MOBA_HEADS * HEAD_DIM
    cast = lambda w: w.astype(MXU_DTYPE)
    w_in, w_uq, w_ukv = cast(w_in), cast(w_uq), cast(w_ukv)
    w_lat = jnp.concatenate([w_in[:, :c1], w_in[:, c0:c1][:, swap]], axis=1)
    k_cols = w_in[:, c1 + mw:c1 + 2 * mw].reshape(D, MOBA_HEADS, HEAD_DIM)[:, :, MOBA_HEAD_PERM].reshape(D, mw)
    w_m = jnp.concatenate([w_in[:, c1:c1 + mw], k_cols, w_in[:, c1 + 2 * mw:c1 + 3 * mw]], axis=1)
    uq = w_uq.reshape(MLA_Q_RANK, MLA_HEADS, HEAD_DIM + MLA_ROPE_DIM)
    rope_cols = uq[:, :, HEAD_DIM:]
    uq = jnp.concatenate([uq, rope_cols[:, :, swap]], axis=-1).reshape(MLA_Q_RANK, MLA_HEADS * 2 * HEAD_DIM)
    ukv = w_ukv.reshape(MLA_KV_RANK, MLA_HEADS, 2, HEAD_DIM).transpose(0, 2, 1, 3).reshape(MLA_KV_RANK, -1)
    return dict(w_lat=w_lat, w_m=w_m, uq=uq, ukv=ukv, wo=cast(w_out), wup=cast(w_up), wdn=cast(w_down))


def kernel(x, positions, w_in, mla_q_norm, mla_kv_norm, w_uq, w_ukv, w_out, ln1_g, ln1_b, w_up, w_down, ln2_g, ln2_b):
    B, S, D = x.shape
    assert S % ATTN_TILE == 0 and w_in.shape[0] == DEPTH
    for l in range(DEPTH):
        w = _layer_weights(w_in[l], w_uq[l], w_ukv[l], w_out[l], w_up[l], w_down[l])
        row = lambda v: v[l].reshape(1, -1)
        qT_a, k_a, vT_a = _mla_prep(x, positions, w["w_lat"], row(mla_q_norm), row(mla_kv_norm), w["uq"], w["ukv"])
        qT_b, k_b, vT_b, kmean = _moba_prep(x, positions, w["w_m"])
        nb = S // ATTN_TILE
        out_a = _attention(qT_a, k_a, vT_a, None, min(MLA_TILES, nb))
        out_b = _attention(qT_b, k_b, vT_b, kmean, min(MOBA_TILES, nb))
        x1 = _outproj_ln(out_a.reshape(B * S, -1), out_b.reshape(B * S, -1), w["wo"],
                         x.reshape(B * S, D), row(ln1_g), row(ln1_b))
        x = _ffn_ln(x1, w["wup"], w["wdn"], row(ln2_g), row(ln2_b)).reshape(B, S, D)
    return x
```
